```python
import math
import jax, jax.numpy as jnp
from jax import lax
import numpy as np

D_MODEL = 1024
BATCH = 32
SEQ = 256
DEPTH = 4
DEC_BATCH = 2
DEC_SEQ = 4096
PAST_LEN = 512

GRID_W = 64
N_HEADS = 8
HEAD_DIM = 64
V_DIM = 2 * HEAD_DIM
ATTN_WIDTH = N_HEADS * V_DIM
ROPE_BASE = 10000.0
Q_BLOCK = 128
N_POOL_GROUPS = 4
POOL_GROUP = D_MODEL // N_POOL_GROUPS
POOL_WINDOWS = (2, 4, 8, 16)
N_EXPERTS = 16
EC_CAPACITY = 2
EXPERT_FF = 1408
N_ATTN_LAYERS = (DEPTH + 1) // 2
N_POOL_LAYERS = DEPTH // 2
EPS = 1e-6

kernel_name = 'hybrid_diffattn_pool_ec_diffusion_step'


def rmsnorm(x, g):
    xf = x.astype(jnp.float32)
    y = xf * lax.rsqrt(jnp.mean(xf * xf, axis=-1, keepdims=True) + EPS)
    return y.astype(x.dtype) * g


def modulation(cvec, w, b):
    m = jax.nn.silu(cvec) @ w + b
    return jnp.split(m, 6, axis=-1)


def adaln(x, g, shift, scale):
    return rmsnorm(x, g) * (1 + scale) + shift


def axial_rope_tables(n_tokens):
    rows = n_tokens // GRID_W
    t = jnp.arange(rows * GRID_W)
    row = (t // GRID_W).astype(jnp.float32)
    col = (t % GRID_W).astype(jnp.float32)
    half = HEAD_DIM // 2
    inv = ROPE_BASE ** (-jnp.arange(0, half, 2, dtype=jnp.float32) / half)
    ang = jnp.concatenate([row[:, None] * inv, col[:, None] * inv], axis=-1)
    return jnp.cos(ang), jnp.sin(ang)


def apply_axial_rope(x, cos, sin):
    q = HEAD_DIM // 4
    cos = cos.astype(x.dtype)[None, :, None, None, :]
    sin = sin.astype(x.dtype)[None, :, None, None, :]
    cr, cc = cos[..., :q], cos[..., q:]
    sr, sc = sin[..., :q], sin[..., q:]
    xr1, xr2 = x[..., :q], x[..., q:2 * q]
    xc1, xc2 = x[..., 2 * q:3 * q], x[..., 3 * q:]
    return jnp.concatenate([xr1 * cr - xr2 * sr, xr1 * sr + xr2 * cr,
                            xc1 * cc - xc2 * sc, xc1 * sc + xc2 * cc], axis=-1)


def project_qkv(h, w):
    b, n, _ = h.shape
    q, k, v = jnp.split(h @ w, 3, axis=-1)
    return (q.reshape(b, n, N_HEADS, 2, HEAD_DIM),
            k.reshape(b, n, N_HEADS, 2, HEAD_DIM),
            v.reshape(b, n, N_HEADS, V_DIM))


def diff_lambda(lq1, lk1, lq2, lk2, lam_init):
    return (jnp.exp(jnp.sum(lq1 * lk1).astype(jnp.float32))
            - jnp.exp(jnp.sum(lq2 * lk2).astype(jnp.float32)) + lam_init)


def diff_attention(q, k, v, lam):
    b, n, h, _, d = q.shape
    nb = n // Q_BLOCK
    qb = (q * (d ** -0.5)).reshape(b, nb, Q_BLOCK, h, 2, d).transpose(1, 0, 2, 3, 4, 5)

    def block(qi):
        s = jnp.einsum('bqhcd,bkhcd->bhcqk', qi, k).astype(jnp.float32)
        p = jax.nn.softmax(s, axis=-1)
        a = p[:, :, 0] - lam * p[:, :, 1]
        return jnp.einsum('bhqk,bkhe->bqhe', a.astype(v.dtype), v)

    o = lax.map(block, qb)
    return o.transpose(1, 0, 2, 3, 4).reshape(b, n, h, V_DIM)


def attn_output(o, g, wo, lam_init):
    b, n = o.shape[:2]
    o = rmsnorm(o, g) * (1.0 - lam_init)
    return o.reshape(b, n, ATTN_WIDTH) @ wo


def multiscale_pool(h, w, bias, scale):
    b, n, _ = h.shape
    hg = h.reshape(b, n, N_POOL_GROUPS, POOL_GROUP)
    t = jnp.arange(n)
    outs = []
    for gi, win in enumerate(POOL_WINDOWS):
        xg = hg[:, :, gi].astype(jnp.float32)
        cs = jnp.concatenate([jnp.zeros((b, 1, POOL_GROUP), jnp.float32),
                              jnp.cumsum(xg, axis=1)], axis=1)
        lo = jnp.clip(t - win // 2, 0, n)
        hi = jnp.clip(t + win // 2, 0, n)
        cnt = (hi - lo).astype(jnp.float32)[None, :, None]
        outs.append((cs[:, hi] - cs[:, lo]) / cnt - xg)
    pooled = jnp.stack(outs, axis=2).astype(h.dtype)
    y = jnp.einsum('bngc,gcd->bngd', pooled, w) + bias
    return y.reshape(b, n, D_MODEL) * scale


def expert_choice_ffn(h, w_router, w_gate, w_up, w_down):
    n, d = h.shape
    cap = EC_CAPACITY * n // N_EXPERTS
    aff = jax.nn.softmax((h @ w_router).astype(jnp.float32), axis=-1)
    g, idx = lax.top_k(aff.T, cap)
    xe = h[idx]
    hid = jax.nn.silu(jnp.einsum('ecd,edf->ecf', xe, w_gate)) * jnp.einsum('ecd,edf->ecf', xe, w_up)
    ye = jnp.einsum('ecf,efd->ecd', hid, w_down) * g[..., None].astype(h.dtype)
    return jnp.zeros_like(h).at[idx.reshape(-1)].add(ye.reshape(-1, d))


def setup_inputs(seed: int = 0) -> dict:
    key = jax.random.key(seed)
    ks = jax.random.split(key, 25)

    def nrm(k, shape, scale=1.0):
        return jax.random.normal(k, shape, jnp.float32) * scale

    return {
        'x_prompt': nrm(ks[0], (BATCH, SEQ, D_MODEL)),
        'x_sample': nrm(ks[1], (DEC_BATCH, DEC_SEQ, D_MODEL)),
        'cache_k': nrm(ks[2], (DEC_BATCH, N_ATTN_LAYERS, PAST_LEN, N_HEADS, 2 * HEAD_DIM)),
        'cache_v': nrm(ks[3], (DEC_BATCH, N_ATTN_LAYERS, PAST_LEN, N_HEADS, V_DIM)),
        'c': nrm(ks[4], (DEC_BATCH, D_MODEL)),
        'c_ctx': nrm(ks[5], (D_MODEL,)),
        'w_mod': nrm(ks[6], (DEPTH, D_MODEL, 6 * D_MODEL), 0.5 * D_MODEL ** -0.5),
        'b_mod': nrm(ks[7], (DEPTH, 6 * D_MODEL), 0.01),
        'norm1_g': 1.0 + nrm(ks[8], (DEPTH, D_MODEL), 0.02),
        'norm2_g': 1.0 + nrm(ks[9], (DEPTH, D_MODEL), 0.02),
        'attn_wqkv': nrm(ks[10], (N_ATTN_LAYERS, D_MODEL, 3 * ATTN_WIDTH), D_MODEL ** -0.5),
        'attn_wo': nrm(ks[11], (N_ATTN_LAYERS, ATTN_WIDTH, D_MODEL), ATTN_WIDTH ** -0.5),
        'lambda_q1': nrm(ks[12], (N_ATTN_LAYERS, HEAD_DIM), 0.1),
        'lambda_k1': nrm(ks[13], (N_ATTN_LAYERS, HEAD_DIM), 0.1),
        'lambda_q2': nrm(ks[14], (N_ATTN_LAYERS, HEAD_DIM), 0.1),
        'lambda_k2': nrm(ks[15], (N_ATTN_LAYERS, HEAD_DIM), 0.1),
        'subln_g': 1.0 + nrm(ks[16], (N_ATTN_LAYERS, V_DIM), 0.02),
        'pool_w': nrm(ks[17], (N_POOL_LAYERS, N_POOL_GROUPS, POOL_GROUP, POOL_GROUP), POOL_GROUP ** -0.5),
        'pool_b': nrm(ks[18], (N_POOL_LAYERS, N_POOL_GROUPS, POOL_GROUP), 0.01),
        'pool_scale': 1.0 + nrm(ks[19], (N_POOL_LAYERS, D_MODEL), 0.02),
        'router_w': nrm(ks[20], (DEPTH, D_MODEL, N_EXPERTS), D_MODEL ** -0.5),
        'expert_w_gate': nrm(ks[21], (DEPTH, N_EXPERTS, D_MODEL, EXPERT_FF), D_MODEL ** -0.5),
        'expert_w_up': nrm(ks[22], (DEPTH, N_EXPERTS, D_MODEL, EXPERT_FF), D_MODEL ** -0.5),
        'expert_w_down': nrm(ks[23], (DEPTH, N_EXPERTS, EXPERT_FF, D_MODEL), EXPERT_FF ** -0.5),
        'final_g': 1.0 + nrm(ks[24], (D_MODEL,), 0.02),
    }


def reference(x_prompt, x_sample, cache_k, cache_v, c, c_ctx, w_mod, b_mod, norm1_g, norm2_g,
              attn_wqkv, attn_wo, lambda_q1, lambda_k1, lambda_q2, lambda_k2, subln_g,
              pool_w, pool_b, pool_scale, router_w, expert_w_gate, expert_w_up, expert_w_down,
              final_g):
    bp, n_ctx = x_prompt.shape[:2]
    bs, n_lat = x_sample.shape[:2]
    past = cache_k.shape[2]
    cos, sin = axial_rope_tables(n_lat)
    xp, xs = x_prompt, x_sample
    new_k, new_v = [], []
    for l in range(DEPTH):
        sp1, cp1, gp1, sp2, cp2, gp2 = modulation(c_ctx, w_mod[l], b_mod[l])
        ss1, cs1, gs1, ss2, cs2, gs2 = modulation(c[:, None, :], w_mod[l], b_mod[l])
        hp = adaln(xp, norm1_g[l], sp1, cp1)
        hs = adaln(xs, norm1_g[l], ss1, cs1)
        if l % 2 == 0:
            a = l // 2
            lam_init = 0.8 - 0.6 * math.exp(-0.3 * l)
            lam = diff_lambda(lambda_q1[a], lambda_k1[a], lambda_q2[a], lambda_k2[a], lam_init)
            qp, kp, vp = project_qkv(hp, attn_wqkv[a])
            new_k.append(kp.reshape(bp, n_ctx, N_HEADS, 2 * HEAD_DIM))
            new_v.append(vp)
            mp = attn_output(diff_attention(qp, kp, vp, lam), subln_g[a], attn_wo[a], lam_init)
            qs, ks_, vs = project_qkv(hs, attn_wqkv[a])
            qs = apply_axial_rope(qs, cos, sin)
            ks_ = apply_axial_rope(ks_, cos, sin)
            kc = cache_k[:, a].reshape(bs, past, N_HEADS, 2, HEAD_DIM)
            k_all = jnp.concatenate([ks_, kc], axis=1)
            v_all = jnp.concatenate([vs, cache_v[:, a]], axis=1)
            ms = attn_output(diff_attention(qs, k_all, v_all, lam), subln_g[a], attn_wo[a], lam_init)
        else:
            p = l // 2
            mp = multiscale_pool(hp, pool_w[p], pool_b[p], pool_scale[p])
            ms = multiscale_pool(hs, pool_w[p], pool_b[p], pool_scale[p])
        xp = xp + gp1 * mp
        xs = xs + gs1 * ms
        hp = adaln(xp, norm2_g[l], sp2, cp2)
        hs = adaln(xs, norm2_g[l], ss2, cs2)
        fp = expert_choice_ffn(hp.reshape(-1, D_MODEL), router_w[l], expert_w_gate[l],
                               expert_w_up[l], expert_w_down[l]).reshape(xp.shape)
        fs = expert_choice_ffn(hs.reshape(-1, D_MODEL), router_w[l], expert_w_gate[l],
                               expert_w_up[l], expert_w_down[l]).reshape(xs.shape)
        xp = xp + gp2 * fp
        xs = xs + gs2 * fs
    y_prompt = rmsnorm(xp, final_g)
    y_sample = rmsnorm(xs, final_g)
    new_cache_k = jnp.stack(new_k, axis=1)
    new_cache_v = jnp.stack(new_v, axis=1)
    return (y_prompt, y_sample, new_cache_k, new_cache_v)
```

```python
import math
import functools
import jax
import jax.numpy as jnp
from jax import lax
from jax.experimental import pallas as pl
from jax.experimental.pallas import tpu as pltpu

D_MODEL = 1024
DEPTH = 4
GRID_W = 64
N_HEADS = 8
HEAD_DIM = 64
V_DIM = 2 * HEAD_DIM
ATTN_WIDTH = N_HEADS * V_DIM
ROPE_BASE = 10000.0
Q_BLOCK = 128
N_POOL_GROUPS = 4
POOL_GROUP = D_MODEL // N_POOL_GROUPS
POOL_WINDOWS = (2, 4, 8, 16)
N_EXPERTS = 16
EC_CAPACITY = 2
EPS = 1e-6


def _rmsnorm(x, g):
    xf = x.astype(jnp.float32)
    y = xf * lax.rsqrt(jnp.mean(xf * xf, axis=-1, keepdims=True) + EPS)
    return y.astype(x.dtype) * g


def _modulation(cvec, w, b):
    m = jax.nn.silu(cvec) @ w + b
    return jnp.split(m, 6, axis=-1)


def _adaln(x, g, shift, scale):
    return _rmsnorm(x, g) * (1 + scale) + shift


def _rope_tables(n_tokens):
    rows = n_tokens // GRID_W
    t = jnp.arange(rows * GRID_W)
    row = (t // GRID_W).astype(jnp.float32)
    col = (t % GRID_W).astype(jnp.float32)
    half = HEAD_DIM // 2
    inv = ROPE_BASE ** (-jnp.arange(0, half, 2, dtype=jnp.float32) / half)
    ang = jnp.concatenate([row[:, None] * inv, col[:, None] * inv], axis=-1)
    return jnp.cos(ang), jnp.sin(ang)


def _rope(x, cos, sin):
    q = HEAD_DIM // 4
    cos = cos.astype(x.dtype)[None, :, None, None, :]
    sin = sin.astype(x.dtype)[None, :, None, None, :]
    cr, cc = cos[..., :q], cos[..., q:]
    sr, sc = sin[..., :q], sin[..., q:]
    xr1, xr2 = x[..., :q], x[..., q:2 * q]
    xc1, xc2 = x[..., 2 * q:3 * q], x[..., 3 * q:]
    return jnp.concatenate([xr1 * cr - xr2 * sr, xr1 * sr + xr2 * cr,
                            xc1 * cc - xc2 * sc, xc1 * sc + xc2 * cc], axis=-1)


def _project_qkv(h, w):
    b, n, _ = h.shape
    q, k, v = jnp.split(h @ w, 3, axis=-1)
    return (q.reshape(b, n, N_HEADS, 2, HEAD_DIM),
            k.reshape(b, n, N_HEADS, 2, HEAD_DIM),
            v.reshape(b, n, N_HEADS, V_DIM))


def _diff_attention(q, k, v, lam):
    b, n, h, _, d = q.shape
    nb = n // Q_BLOCK
    qb = (q * (d ** -0.5)).reshape(b, nb, Q_BLOCK, h, 2, d).transpose(1, 0, 2, 3, 4, 5)

    def block(qi):
        s = jnp.einsum('bqhcd,bkhcd->bhcqk', qi, k).astype(jnp.float32)
        p = jax.nn.softmax(s, axis=-1)
        a = p[:, :, 0] - lam * p[:, :, 1]
        return jnp.einsum('bhqk,bkhe->bqhe', a.astype(v.dtype), v)

    o = lax.map(block, qb)
    return o.transpose(1, 0, 2, 3, 4).reshape(b, n, h, V_DIM)


def _attn_output(o, g, wo, lam_init):
    b, n = o.shape[:2]
    o = _rmsnorm(o, g) * (1.0 - lam_init)
    return o.reshape(b, n, ATTN_WIDTH) @ wo


def _pool(h, w, bias, scale):
    b, n, _ = h.shape
    hg = h.reshape(b, n, N_POOL_GROUPS, POOL_GROUP)
    t = jnp.arange(n)
    outs = []
    for gi, win in enumerate(POOL_WINDOWS):
        xg = hg[:, :, gi].astype(jnp.float32)
        cs = jnp.concatenate([jnp.zeros((b, 1, POOL_GROUP), jnp.float32),
                              jnp.cumsum(xg, axis=1)], axis=1)
        lo = jnp.clip(t - win // 2, 0, n)
        hi = jnp.clip(t + win // 2, 0, n)
        cnt = (hi - lo).astype(jnp.float32)[None, :, None]
        outs.append((cs[:, hi] - cs[:, lo]) / cnt - xg)
    pooled = jnp.stack(outs, axis=2).astype(h.dtype)
    y = jnp.einsum('bngc,gcd->bngd', pooled, w) + bias
    return y.reshape(b, n, D_MODEL) * scale


def _ec_ffn(h, w_router, w_gate, w_up, w_down):
    n, d = h.shape
    cap = EC_CAPACITY * n // N_EXPERTS
    aff = jax.nn.softmax((h @ w_router).astype(jnp.float32), axis=-1)
    g, idx = lax.top_k(aff.T, cap)
    xe = h[idx]
    hid = jax.nn.silu(jnp.einsum('ecd,edf->ecf', xe, w_gate)) * jnp.einsum('ecd,edf->ecf', xe, w_up)
    ye = jnp.einsum('ecf,efd->ecd', hid, w_down) * g[..., None].astype(h.dtype)
    return jnp.zeros_like(h).at[idx.reshape(-1)].add(ye.reshape(-1, d))


def _final_norm_kernel(x_ref, g_ref, o_ref):
    x = x_ref[...]
    ms = jnp.mean(x * x, axis=-1, keepdims=True)
    o_ref[...] = x * lax.rsqrt(ms + EPS) * g_ref[...]


def _final_norm(x, g):
    n, d = x.shape
    tb = 512
    return pl.pallas_call(
        _final_norm_kernel,
        grid=(n // tb,),
        in_specs=[pl.BlockSpec((tb, d), lambda i: (i, 0)),
                  pl.BlockSpec((1, d), lambda i: (0, 0))],
        out_specs=pl.BlockSpec((tb, d), lambda i: (i, 0)),
        out_shape=jax.ShapeDtypeStruct((n, d), x.dtype),
    )(x, g.reshape(1, d))


def kernel(x_prompt, x_sample, cache_k, cache_v, c, c_ctx, w_mod, b_mod, norm1_g, norm2_g,
           attn_wqkv, attn_wo, lambda_q1, lambda_k1, lambda_q2, lambda_k2, subln_g,
           pool_w, pool_b, pool_scale, router_w, expert_w_gate, expert_w_up, expert_w_down,
           final_g):
    bp, n_ctx = x_prompt.shape[:2]
    bs, n_lat = x_sample.shape[:2]
    past = cache_k.shape[2]
    cos, sin = _rope_tables(n_lat)
    xp, xs = x_prompt, x_sample
    new_k, new_v = [], []
    for l in range(DEPTH):
        sp1, cp1, gp1, sp2, cp2, gp2 = _modulation(c_ctx, w_mod[l], b_mod[l])
        ss1, cs1, gs1, ss2, cs2, gs2 = _modulation(c[:, None, :], w_mod[l], b_mod[l])
        hp = _adaln(xp, norm1_g[l], sp1, cp1)
        hs = _adaln(xs, norm1_g[l], ss1, cs1)
        if l % 2 == 0:
            a = l // 2
            lam_init = 0.8 - 0.6 * math.exp(-0.3 * l)
            lam = (jnp.exp(jnp.sum(lambda_q1[a] * lambda_k1[a]))
                   - jnp.exp(jnp.sum(lambda_q2[a] * lambda_k2[a])) + lam_init)
            qp, kp, vp = _project_qkv(hp, attn_wqkv[a])
            new_k.append(kp.reshape(bp, n_ctx, N_HEADS, 2 * HEAD_DIM))
            new_v.append(vp)
            mp = _attn_output(_diff_attention(qp, kp, vp, lam), subln_g[a], attn_wo[a], lam_init)
            qs, ks_, vs = _project_qkv(hs, attn_wqkv[a])
            qs = _rope(qs, cos, sin)
            ks_ = _rope(ks_, cos, sin)
            kc = cache_k[:, a].reshape(bs, past, N_HEADS, 2, HEAD_DIM)
            k_all = jnp.concatenate([ks_, kc], axis=1)
            v_all = jnp.concatenate([vs, cache_v[:, a]], axis=1)
            ms = _attn_output(_diff_attention(qs, k_all, v_all, lam), subln_g[a], attn_wo[a], lam_init)
        else:
            p = l // 2
            mp = _pool(hp, pool_w[p], pool_b[p], pool_scale[p])
            ms = _pool(hs, pool_w[p], pool_b[p], pool_scale[p])
        xp = xp + gp1 * mp
        xs = xs + gs1 * ms
        hp = _adaln(xp, norm2_g[l], sp2, cp2)
        hs = _adaln(xs, norm2_g[l], ss2, cs2)
        fp = _ec_ffn(hp.reshape(-1, D_MODEL), router_w[l], expert_w_gate[l],
                     expert_w_up[l], expert_w_down[l]).reshape(xp.shape)
        fs = _ec_ffn(hs.reshape(-1, D_MODEL), router_w[l], expert_w_gate[l],
                     expert_w_up[l], expert_w_down[l]).reshape(xs.shape)
        xp = xp + gp2 * fp
        xs = xs + gs2 * fs
    y_prompt = _final_norm(xp.reshape(-1, D_MODEL), final_g).reshape(xp.shape)
    y_sample = _final_norm(xs.reshape(-1, D_MODEL), final_g).reshape(xs.shape)
    return (y_prompt, y_sample, jnp.stack(new_k, axis=1), jnp.stack(new_v, axis=1))
```

```python
import math
import functools
import jax
import jax.numpy as jnp
from jax import lax
from jax.experimental import pallas as pl
from jax.experimental.pallas import tpu as pltpu

D_MODEL = 1024
DEPTH = 4
GRID_W = 64
N_HEADS = 8
HEAD_DIM = 64
V_DIM = 2 * HEAD_DIM
ATTN_WIDTH = N_HEADS * V_DIM
ROPE_BASE = 10000.0
N_POOL_GROUPS = 4
POOL_GROUP = D_MODEL // N_POOL_GROUPS
POOL_WINDOWS = (2, 4, 8, 16)
N_EXPERTS = 16
EC_CAPACITY = 2
EPS = 1e-6

F32 = jnp.float32
BF16 = jnp.bfloat16
VMEM_LIMIT = 56 * 1024 * 1024


def _cparams(sem):
    return pltpu.CompilerParams(dimension_semantics=sem, vmem_limit_bytes=VMEM_LIMIT)


def _mod_kernel(c_ref, w_ref, b_ref, o_ref):
    c = c_ref[...]
    s = (c * (1.0 / (1.0 + jnp.exp(-c)))).astype(BF16)
    o_ref[...] = jnp.dot(s, w_ref[...].astype(BF16), preferred_element_type=F32) + b_ref[...]


def _modulation_all(cvecs, w_mod, b_mod):
    r, d = cvecs.shape
    depth, _, n6 = w_mod.shape
    tn = 1024
    return pl.pallas_call(
        _mod_kernel,
        grid=(depth, n6 // tn),
        in_specs=[pl.BlockSpec((r, d), lambda l, j: (0, 0)),
                  pl.BlockSpec((None, d, tn), lambda l, j: (l, 0, j)),
                  pl.BlockSpec((None, 1, tn), lambda l, j: (l, 0, j))],
        out_specs=pl.BlockSpec((None, r, tn), lambda l, j: (l, 0, j)),
        out_shape=jax.ShapeDtypeStruct((depth, r, n6), F32),
        compiler_params=_cparams(("parallel", "parallel")),
        name="modulation",
    )(cvecs, w_mod, b_mod.reshape(depth, 1, n6))


def _adaln_rows(x, g, shift, scale):
    ms = jnp.mean(x * x, axis=-1, keepdims=True)
    return (x * lax.rsqrt(ms + EPS)) * g * (1.0 + scale) + shift


def _swap16(x):
    lane = lax.broadcasted_iota(jnp.int32, x.shape, 1)
    up = pltpu.roll(x, x.shape[1] - 16, axis=1)
    dn = pltpu.roll(x, 16, axis=1)
    return jnp.where((lane % 32) < 16, up, dn)


def _qkv_kernel(*refs, rope, kv_dtype):
    if rope:
        x_ref, g_ref, sh_ref, sc_ref, w_ref, cos_ref, sin_ref, q_ref, k_ref, v_ref = refs
    else:
        x_ref, g_ref, sh_ref, sc_ref, w_ref, q_ref, k_ref, v_ref = refs
    h = _adaln_rows(x_ref[...], g_ref[...], sh_ref[...], sc_ref[...]).astype(BF16)
    width = q_ref.shape[1]
    qscale = HEAD_DIM ** -0.5
    for part, o_ref in enumerate((q_ref, k_ref, v_ref)):
        r = jnp.dot(h, w_ref[:, part * width:(part + 1) * width], preferred_element_type=F32)
        if part < 2 and rope:
            c = cos_ref[...]
            s = sin_ref[...]
            for hd in range(width // 128):
                blk = r[:, hd * 128:(hd + 1) * 128]
                blk = blk * c + _swap16(blk) * s
                if part == 0:
                    blk = blk * qscale
                o_ref[:, hd * 128:(hd + 1) * 128] = blk.astype(o_ref.dtype)
        else:
            if part == 0:
                r = r * qscale
            o_ref[...] = r.astype(o_ref.dtype)


def _qkv_project(x2d, g, shift, scale, w_bf16, rope_tabs, tokens_per_mod, kv_dtype, tm=512):
    n, d = x2d.shape
    width = w_bf16.shape[1] // 3
    rope = rope_tabs is not None
    blocks_per_mod = tokens_per_mod // tm
    in_specs = [pl.BlockSpec((tm, d), lambda i: (i, 0)),
                pl.BlockSpec((1, d), lambda i: (0, 0)),
                pl.BlockSpec((None, 1, d), lambda i: (i // blocks_per_mod, 0, 0)),
                pl.BlockSpec((None, 1, d), lambda i: (i // blocks_per_mod, 0, 0)),
                pl.BlockSpec((d, 3 * width), lambda i: (0, 0))]
    args = [x2d, g.reshape(1, d), shift, scale, w_bf16]
    if rope:
        cos_t, sin_t = rope_tabs
        seq_blocks = cos_t.shape[0] // tm
        in_specs += [pl.BlockSpec((tm, 128), lambda i: (i % seq_blocks, 0)),
                     pl.BlockSpec((tm, 128), lambda i: (i % seq_blocks, 0))]
        args += [cos_t, sin_t]
    out_spec = pl.BlockSpec((tm, width), lambda i: (i, 0))
    return pl.pallas_call(
        functools.partial(_qkv_kernel, rope=rope, kv_dtype=kv_dtype),
        grid=(n // tm,),
        in_specs=in_specs,
        out_specs=[out_spec, out_spec, out_spec],
        out_shape=[jax.ShapeDtypeStruct((n, width), BF16),
                   jax.ShapeDtypeStruct((n, width), kv_dtype),
                   jax.ShapeDtypeStruct((n, width), kv_dtype)],
        compiler_params=_cparams(("parallel",)),
        name="adaln_qkv_rope" if rope else "adaln_qkv",
    )(*args)


def _attn_kernel(lamv_ref, q_ref, k_ref, v_ref, g_ref, o_ref, qs_ref, m_ref, l_ref, acc_ref,
                 *, tk, lam_init):
    tq = q_ref.shape[0]
    nk = k_ref.shape[0]
    q = q_ref[...]
    lane = lax.broadcasted_iota(jnp.int32, q.shape, 1)
    zero = jnp.zeros_like(q)
    qs_ref[0:tq, :] = jnp.where(lane < HEAD_DIM, q, zero)
    qs_ref[tq:2 * tq, :] = jnp.where(lane >= HEAD_DIM, q, zero)
    m_ref[...] = jnp.full(m_ref.shape, -jnp.inf, F32)
    l_ref[...] = jnp.zeros(l_ref.shape, F32)
    acc_ref[...] = jnp.zeros(acc_ref.shape, F32)

    def body(j, carry):
        off = pl.multiple_of(j * tk, tk)
        kc = k_ref[pl.ds(off, tk), :].astype(BF16)
        vc = v_ref[pl.ds(off, tk), :].astype(BF16)
        s = lax.dot_general(qs_ref[...], kc, (((1,), (1,)), ((), ())),
                            preferred_element_type=F32)
        m_old = m_ref[...]
        m_new = jnp.maximum(m_old, jnp.max(s, axis=-1, keepdims=True))
        alpha = jnp.exp(m_old - m_new)
        p = jnp.exp(s - m_new)
        l_ref[...] = alpha * l_ref[...] + jnp.sum(p, axis=-1, keepdims=True)
        acc_ref[...] = alpha * acc_ref[...] + jnp.dot(p.astype(BF16), vc,
                                                      preferred_element_type=F32)
        m_ref[...] = m_new
        return carry

    lax.fori_loop(0, nk // tk, body, 0)

    lv = lamv_ref[...]
    lam = (jnp.exp(jnp.sum(lv[0:1, :] * lv[1:2, :], axis=-1, keepdims=True))
           - jnp.exp(jnp.sum(lv[2:3, :] * lv[3:4, :], axis=-1, keepdims=True)) + lam_init)
    o = acc_ref[...] / l_ref[...]
    d = o[0:tq, :] - lam * o[tq:2 * tq, :]
    ms = jnp.mean(d * d, axis=-1, keepdims=True)
    o_ref[...] = ((d * lax.rsqrt(ms + EPS)) * g_ref[...] * (1.0 - lam_init)).astype(o_ref.dtype)


def _diff_attention(q, k, v, lamv, subln_g, lam_init, batch, tq, tk):
    nq = q.shape[0] // batch
    nk = k.shape[0] // batch
    q3 = q.reshape(batch, nq, ATTN_WIDTH)
    k3 = k.reshape(batch, nk, ATTN_WIDTH)
    v3 = v.reshape(batch, nk, ATTN_WIDTH)
    out = pl.pallas_call(
        functools.partial(_attn_kernel, tk=tk, lam_init=lam_init),
        grid=(batch, N_HEADS, nq // tq),
        in_specs=[pl.BlockSpec((4, HEAD_DIM), lambda b, h, i: (0, 0)),
                  pl.BlockSpec((None, tq, V_DIM), lambda b, h, i: (b, i, h)),
                  pl.BlockSpec((None, nk, V_DIM), lambda b, h, i: (b, 0, h)),
                  pl.BlockSpec((None, nk, V_DIM), lambda b, h, i: (b, 0, h)),
                  pl.BlockSpec((1, V_DIM), lambda b, h, i: (0, 0))],
        out_specs=pl.BlockSpec((None, tq, V_DIM), lambda b, h, i: (b, i, h)),
        out_shape=jax.ShapeDtypeStruct((batch, nq, ATTN_WIDTH), BF16),
        scratch_shapes=[pltpu.VMEM((2 * tq, V_DIM), BF16),
                        pltpu.VMEM((2 * tq, 1), F32),
                        pltpu.VMEM((2 * tq, 1), F32),
                        pltpu.VMEM((2 * tq, V_DIM), F32)],
        compiler_params=_cparams(("parallel", "parallel", "parallel")),
        name="diff_attention",
    )(lamv, q3, k3, v3, subln_g.reshape(1, V_DIM))
    return out.reshape(batch * nq, ATTN_WIDTH)


def _proj_res_kernel(o_ref, w_ref, x_ref, gate_ref, xo_ref):
    y = jnp.dot(o_ref[...], w_ref[...], preferred_element_type=F32)
    xo_ref[...] = x_ref[...] + gate_ref[...] * y


def _proj_residual(o, w_bf16, x2d, gate, tokens_per_mod, tm=512):
    n, d = x2d.shape
    kdim = o.shape[1]
    blocks_per_mod = tokens_per_mod // tm
    return pl.pallas_call(
        _proj_res_kernel,
        grid=(n // tm,),
        in_specs=[pl.BlockSpec((tm, kdim), lambda i: (i, 0)),
                  pl.BlockSpec((kdim, d), lambda i: (0, 0)),
                  pl.BlockSpec((tm, d), lambda i: (i, 0)),
                  pl.BlockSpec((None, 1, d), lambda i: (i // blocks_per_mod, 0, 0))],
        out_specs=pl.BlockSpec((tm, d), lambda i: (i, 0)),
        out_shape=jax.ShapeDtypeStruct((n, d), F32),
        compiler_params=_cparams(("parallel",)),
        name="proj_residual",
    )(o, w_bf16, x2d, gate)


def _rmsnorm(x, g):
    y = x * lax.rsqrt(jnp.mean(x * x, axis=-1, keepdims=True) + EPS)
    return y * g


def _pool(h, w, bias, scale):
    b, n, _ = h.shape
    hg = h.reshape(b, n, N_POOL_GROUPS, POOL_GROUP)
    t = jnp.arange(n)
    outs = []
    for gi, win in enumerate(POOL_WINDOWS):
        xg = hg[:, :, gi].astype(jnp.float32)
        cs = jnp.concatenate([jnp.zeros((b, 1, POOL_GROUP), jnp.float32),
                              jnp.cumsum(xg, axis=1)], axis=1)
        lo = jnp.clip(t - win // 2, 0, n)
        hi = jnp.clip(t + win // 2, 0, n)
        cnt = (hi - lo).astype(jnp.float32)[None, :, None]
        outs.append((cs[:, hi] - cs[:, lo]) / cnt - xg)
    pooled = jnp.stack(outs, axis=2).astype(h.dtype)
    y = jnp.einsum('bngc,gcd->bngd', pooled, w) + bias
    return y.reshape(b, n, D_MODEL) * scale


def _ec_ffn(h, w_router, w_gate, w_up, w_down):
    n, d = h.shape
    cap = EC_CAPACITY * n // N_EXPERTS
    aff = jax.nn.softmax((h @ w_router).astype(jnp.float32), axis=-1)
    g, idx = lax.top_k(aff.T, cap)
    xe = h[idx]
    hid = jax.nn.silu(jnp.einsum('ecd,edf->ecf', xe, w_gate)) * jnp.einsum('ecd,edf->ecf', xe, w_up)
    ye = jnp.einsum('ecf,efd->ecd', hid, w_down) * g[..., None].astype(h.dtype)
    return jnp.zeros_like(h).at[idx.reshape(-1)].add(ye.reshape(-1, d))


def _final_norm_kernel(x_ref, g_ref, o_ref):
    x = x_ref[...]
    ms = jnp.mean(x * x, axis=-1, keepdims=True)
    o_ref[...] = x * lax.rsqrt(ms + EPS) * g_ref[...]


def _final_norm(x, g, tb=512):
    n, d = x.shape
    return pl.pallas_call(
        _final_norm_kernel,
        grid=(n // tb,),
        in_specs=[pl.BlockSpec((tb, d), lambda i: (i, 0)),
                  pl.BlockSpec((1, d), lambda i: (0, 0))],
        out_specs=pl.BlockSpec((tb, d), lambda i: (i, 0)),
        out_shape=jax.ShapeDtypeStruct((n, d), x.dtype),
        compiler_params=_cparams(("parallel",)),
        name="final_norm",
    )(x, g.reshape(1, d))


def _rope_tables(n_tokens):
    t = jnp.arange(n_tokens)
    row = (t // GRID_W).astype(F32)
    col = (t % GRID_W).astype(F32)
    half = HEAD_DIM // 2
    inv = ROPE_BASE ** (-jnp.arange(0, half, 2, dtype=F32) / half)
    ar, ac = row[:, None] * inv, col[:, None] * inv
    c64 = jnp.concatenate([jnp.cos(ar), jnp.cos(ar), jnp.cos(ac), jnp.cos(ac)], axis=-1)
    s64 = jnp.concatenate([-jnp.sin(ar), jnp.sin(ar), -jnp.sin(ac), jnp.sin(ac)], axis=-1)
    return jnp.tile(c64, (1, 2)), jnp.tile(s64, (1, 2))


def kernel(x_prompt, x_sample, cache_k, cache_v, c, c_ctx, w_mod, b_mod, norm1_g, norm2_g,
           attn_wqkv, attn_wo, lambda_q1, lambda_k1, lambda_q2, lambda_k2, subln_g,
           pool_w, pool_b, pool_scale, router_w, expert_w_gate, expert_w_up, expert_w_down,
           final_g):
    bp, n_ctx, d = x_prompt.shape
    bs, n_lat, _ = x_sample.shape
    past = cache_k.shape[2]
    n_p, n_s = bp * n_ctx, bs * n_lat

    rows = 8 * ((1 + bs + 7) // 8)
    cvecs = jnp.zeros((rows, d), F32).at[0].set(c_ctx).at[1:1 + bs].set(c)
    mod = _modulation_all(cvecs, w_mod, b_mod)
    rope_tabs = _rope_tables(n_lat)

    xp = x_prompt.reshape(n_p, d)
    xs = x_sample.reshape(n_s, d)
    new_k, new_v = [], []
    for l in range(DEPTH):
        mp_ = mod[l, 0:1].reshape(1, 6, 1, d)
        ms_ = mod[l, 1:1 + bs].reshape(bs, 6, 1, d)
        sp1, cp1, gp1, sp2, cp2, gp2 = [mp_[:, i] for i in range(6)]
        ss1, cs1, gs1, ss2, cs2, gs2 = [ms_[:, i] for i in range(6)]
        if l % 2 == 0:
            a = l // 2
            lam_init = 0.8 - 0.6 * math.exp(-0.3 * l)
            lamv = jnp.stack([lambda_q1[a], lambda_k1[a], lambda_q2[a], lambda_k2[a]])
            wqkv = attn_wqkv[a].astype(BF16)
            wo = attn_wo[a].astype(BF16)
            qp, kp, vp = _qkv_project(xp, norm1_g[l], sp1, cp1, wqkv, None, n_p, F32)
            new_k.append(kp.reshape(bp, n_ctx, N_HEADS, 2 * HEAD_DIM))
            new_v.append(vp.reshape(bp, n_ctx, N_HEADS, V_DIM))
            op = _diff_attention(qp, kp, vp, lamv, subln_g[a], lam_init, bp, n_ctx, n_ctx)
            xp = _proj_residual(op, wo, xp, gp1, n_p)
            qs, ks_, vs = _qkv_project(xs, norm1_g[l], ss1, cs1, wqkv, rope_tabs, n_lat, BF16)
            k_all = jnp.concatenate([ks_.reshape(bs, n_lat, ATTN_WIDTH),
                                     cache_k[:, a].reshape(bs, past, ATTN_WIDTH).astype(BF16)],
                                    axis=1).reshape(bs * (n_lat + past), ATTN_WIDTH)
            v_all = jnp.concatenate([vs.reshape(bs, n_lat, ATTN_WIDTH),
                                     cache_v[:, a].reshape(bs, past, ATTN_WIDTH).astype(BF16)],
                                    axis=1).reshape(bs * (n_lat + past), ATTN_WIDTH)
            os_ = _diff_attention(qs, k_all, v_all, lamv, subln_g[a], lam_init, bs, 256, 512)
            xs = _proj_residual(os_, wo, xs, gs1, n_lat)
        else:
            p = l // 2
            hp = (_rmsnorm(xp, norm1_g[l]).reshape(1, n_p, d) * (1 + cp1) + sp1)
            hs = (_rmsnorm(xs, norm1_g[l]).reshape(bs, n_lat, d) * (1 + cs1) + ss1)
            mp = _pool(hp.reshape(bp, n_ctx, d), pool_w[p], pool_b[p], pool_scale[p])
            ms = _pool(hs, pool_w[p], pool_b[p], pool_scale[p])
            xp = (xp.reshape(1, n_p, d) + gp1 * mp.reshape(1, n_p, d)).reshape(n_p, d)
            xs = (xs.reshape(bs, n_lat, d) + gs1 * ms).reshape(n_s, d)
        hp = (_rmsnorm(xp, norm2_g[l]).reshape(1, n_p, d) * (1 + cp2) + sp2).reshape(n_p, d)
        hs = (_rmsnorm(xs, norm2_g[l]).reshape(bs, n_lat, d) * (1 + cs2) + ss2).reshape(n_s, d)
        fp = _ec_ffn(hp, router_w[l], expert_w_gate[l], expert_w_up[l], expert_w_down[l])
        fs = _ec_ffn(hs, router_w[l], expert_w_gate[l], expert_w_up[l], expert_w_down[l])
        xp = (xp.reshape(1, n_p, d) + gp2 * fp.reshape(1, n_p, d)).reshape(n_p, d)
        xs = (xs.reshape(bs, n_lat, d) + gs2 * fs.reshape(bs, n_lat, d)).reshape(n_s, d)
    y_prompt = _final_norm(xp, final_g).reshape(bp, n_ctx, d)
    y_sample = _final_norm(xs, final_g).reshape(bs, n_lat, d)
    return (y_prompt, y_sample, jnp.stack(new_k, axis=1), jnp.stack(new_v, axis=1))
```

```python
import math
import functools
import jax
import jax.numpy as jnp
from jax import lax
from jax.experimental import pallas as pl
from jax.experimental.pallas import tpu as pltpu

D_MODEL = 1024
DEPTH = 4
GRID_W = 64
N_HEADS = 8
HEAD_DIM = 64
V_DIM = 2 * HEAD_DIM
ATTN_WIDTH = N_HEADS * V_DIM
ROPE_BASE = 10000.0
N_POOL_GROUPS = 4
POOL_GROUP = D_MODEL // N_POOL_GROUPS
POOL_WINDOWS = (2, 4, 8, 16)
POOL_HALO = 8
N_EXPERTS = 16
EC_CAPACITY = 2
EPS = 1e-6

F32 = jnp.float32
BF16 = jnp.bfloat16
LANES = 128
LOG2E = 1.4426950408889634
VMEM_LIMIT = 56 * 1024 * 1024
ROW_TILE = 512
ATTN_TQ = 256
ATTN_TK = 1536
ATTN_HEADS_PER_STEP = 2
FFN_ROWS = 512


def _cparams(sem):
    return pltpu.CompilerParams(dimension_semantics=sem, vmem_limit_bytes=VMEM_LIMIT)


def _mod_kernel(c_ref, w_ref, b_ref, o_ref):
    c = c_ref[...]
    s = (c * (1.0 / (1.0 + jnp.exp(-c)))).astype(BF16)
    o_ref[...] = jnp.dot(s, w_ref[...].astype(BF16), preferred_element_type=F32) + b_ref[...]


def _modulation_all(cvecs, w_mod, b_mod):
    r, d = cvecs.shape
    depth, _, n6 = w_mod.shape
    tn = 1024
    return pl.pallas_call(
        _mod_kernel,
        grid=(depth, n6 // tn),
        in_specs=[pl.BlockSpec((r, d), lambda l, j: (0, 0)),
                  pl.BlockSpec((None, d, tn), lambda l, j: (l, 0, j)),
                  pl.BlockSpec((None, 1, tn), lambda l, j: (l, 0, j))],
        out_specs=pl.BlockSpec((None, r, tn), lambda l, j: (l, 0, j)),
        out_shape=jax.ShapeDtypeStruct((depth, r, n6), F32),
        compiler_params=_cparams(("parallel", "parallel")),
        name="modulation",
    )(cvecs, w_mod, b_mod.reshape(depth, 1, n6))


def _adaln_rows(x, g, shift, scale):
    ms = jnp.mean(x * x, axis=-1, keepdims=True)
    return (x * lax.rsqrt(ms + EPS)) * g * (1.0 + scale) + shift


def _swap16(x):
    lane = lax.broadcasted_iota(jnp.int32, x.shape, 1)
    up = pltpu.roll(x, x.shape[1] - 16, axis=1)
    dn = pltpu.roll(x, 16, axis=1)
    return jnp.where((lane % 32) < 16, up, dn)


def _qkv_kernel(*refs, rope):
    if rope:
        x_ref, g_ref, sh_ref, sc_ref, w_ref, cos_ref, sin_ref, q_ref, k_ref, v_ref = refs
    else:
        x_ref, g_ref, sh_ref, sc_ref, w_ref, q_ref, k_ref, v_ref = refs
    h = _adaln_rows(x_ref[...], g_ref[...], sh_ref[...], sc_ref[...]).astype(BF16)
    width = q_ref.shape[1]
    qscale = (HEAD_DIM ** -0.5) * LOG2E
    for part, o_ref in enumerate((q_ref, k_ref, v_ref)):
        r = jnp.dot(h, w_ref[:, part * width:(part + 1) * width], preferred_element_type=F32)
        if part < 2 and rope:
            c = cos_ref[...]
            s = sin_ref[...]
            for hd in range(width // LANES):
                blk = r[:, hd * LANES:(hd + 1) * LANES]
                blk = blk * c + _swap16(blk) * s
                if part == 0:
                    blk = blk * qscale
                o_ref[:, hd * LANES:(hd + 1) * LANES] = blk.astype(o_ref.dtype)
        else:
            if part == 0:
                r = r * qscale
            o_ref[...] = r.astype(o_ref.dtype)


def _qkv_project(x2d, g, shift, scale, w_bf16, rope_tabs, tokens_per_mod, kv_dtype):
    n, d = x2d.shape
    tm = ROW_TILE
    width = w_bf16.shape[1] // 3
    rope = rope_tabs is not None
    blocks_per_mod = tokens_per_mod // tm
    in_specs = [pl.BlockSpec((tm, d), lambda i: (i, 0)),
                pl.BlockSpec((1, d), lambda i: (0, 0)),
                pl.BlockSpec((None, 1, d), lambda i: (i // blocks_per_mod, 0, 0)),
                pl.BlockSpec((None, 1, d), lambda i: (i // blocks_per_mod, 0, 0)),
                pl.BlockSpec((d, 3 * width), lambda i: (0, 0))]
    args = [x2d, g.reshape(1, d), shift, scale, w_bf16]
    if rope:
        cos_t, sin_t = rope_tabs
        seq_blocks = cos_t.shape[0] // tm
        in_specs += [pl.BlockSpec((tm, LANES), lambda i: (i % seq_blocks, 0)),
                     pl.BlockSpec((tm, LANES), lambda i: (i % seq_blocks, 0))]
        args += [cos_t, sin_t]
    out_spec = pl.BlockSpec((tm, width), lambda i: (i, 0))
    return pl.pallas_call(
        functools.partial(_qkv_kernel, rope=rope),
        grid=(n // tm,),
        in_specs=in_specs,
        out_specs=[out_spec, out_spec, out_spec],
        out_shape=[jax.ShapeDtypeStruct((n, width), BF16),
                   jax.ShapeDtypeStruct((n, width), kv_dtype),
                   jax.ShapeDtypeStruct((n, width), kv_dtype)],
        compiler_params=_cparams(("parallel",)),
        name="adaln_qkv_rope" if rope else "adaln_qkv",
    )(*args)


def _attn_kernel(lamv_ref, q_ref, k_ref, v_ref, g_ref, o_ref, qs_ref, m_ref, l_ref, acc_ref,
                 *, tk, lam_init, hp):
    tq = q_ref.shape[0]
    nk = k_ref.shape[0]
    for h in range(hp):
        q = q_ref[:, h * LANES:(h + 1) * LANES]
        lane = lax.broadcasted_iota(jnp.int32, q.shape, 1)
        zero = jnp.zeros_like(q)
        qs_ref[h, 0:tq, :] = jnp.where(lane < HEAD_DIM, q, zero)
        qs_ref[h, tq:2 * tq, :] = jnp.where(lane >= HEAD_DIM, q, zero)
    m_ref[...] = jnp.full(m_ref.shape, -jnp.inf, F32)
    l_ref[...] = jnp.zeros(l_ref.shape, F32)
    acc_ref[...] = jnp.zeros(acc_ref.shape, F32)

    def body(j, carry):
        off = pl.multiple_of(j * tk, tk)
        for h in range(hp):
            kc = k_ref[pl.ds(off, tk), h * LANES:(h + 1) * LANES].astype(BF16)
            vc = v_ref[pl.ds(off, tk), h * LANES:(h + 1) * LANES].astype(BF16)
            s = lax.dot_general(qs_ref[h], kc, (((1,), (1,)), ((), ())),
                                preferred_element_type=F32)
            m_old = m_ref[h]
            m_new = jnp.maximum(m_old, jnp.max(s, axis=1)[:, None])
            alpha = jnp.exp2(m_old - m_new)
            p = jnp.exp2(s - jnp.tile(m_new, (1, tk // LANES)))
            l_ref[h] = alpha * l_ref[h] + jnp.sum(p, axis=1)[:, None]
            acc_ref[h] = alpha * acc_ref[h] + jnp.dot(p.astype(BF16), vc,
                                                      preferred_element_type=F32)
            m_ref[h] = m_new
        return carry

    lax.fori_loop(0, nk // tk, body, 0)

    lv = lamv_ref[...]
    lam = (jnp.exp(jnp.sum(lv[0:1, :] * lv[1:2, :], axis=-1, keepdims=True))
           - jnp.exp(jnp.sum(lv[2:3, :] * lv[3:4, :], axis=-1, keepdims=True)) + lam_init)
    for h in range(hp):
        o = acc_ref[h] / l_ref[h]
        d = o[0:tq, :] - lam * o[tq:2 * tq, :]
        ms = jnp.mean(d * d, axis=-1, keepdims=True)
        o_ref[:, h * LANES:(h + 1) * LANES] = (
            (d * lax.rsqrt(ms + EPS)) * g_ref[...] * (1.0 - lam_init)).astype(o_ref.dtype)


def _diff_attention(q, k, v, lamv, subln_g, lam_init, batch):
    nq = q.shape[0] // batch
    nk = k.shape[0] // batch
    tq = min(ATTN_TQ, nq)
    tk = ATTN_TK if nk % ATTN_TK == 0 else nk
    assert nq % tq == 0 and nk % tk == 0 and tk % LANES == 0
    hp = ATTN_HEADS_PER_STEP
    w = hp * LANES
    q3 = q.reshape(batch, nq, ATTN_WIDTH)
    k3 = k.reshape(batch, nk, ATTN_WIDTH)
    v3 = v.reshape(batch, nk, ATTN_WIDTH)
    out = pl.pallas_call(
        functools.partial(_attn_kernel, tk=tk, lam_init=lam_init, hp=hp),
        grid=(batch, N_HEADS // hp, nq // tq),
        in_specs=[pl.BlockSpec((4, HEAD_DIM), lambda b, h, i: (0, 0)),
                  pl.BlockSpec((None, tq, w), lambda b, h, i: (b, i, h)),
                  pl.BlockSpec((None, nk, w), lambda b, h, i: (b, 0, h)),
                  pl.BlockSpec((None, nk, w), lambda b, h, i: (b, 0, h)),
                  pl.BlockSpec((1, V_DIM), lambda b, h, i: (0, 0))],
        out_specs=pl.BlockSpec((None, tq, w), lambda b, h, i: (b, i, h)),
        out_shape=jax.ShapeDtypeStruct((batch, nq, ATTN_WIDTH), BF16),
        scratch_shapes=[pltpu.VMEM((hp, 2 * tq, V_DIM), BF16),
                        pltpu.VMEM((hp, 2 * tq, LANES), F32),
                        pltpu.VMEM((hp, 2 * tq, LANES), F32),
                        pltpu.VMEM((hp, 2 * tq, V_DIM), F32)],
        compiler_params=_cparams(("parallel", "parallel", "parallel")),
        name="diff_attention",
    )(lamv, q3, k3, v3, subln_g.reshape(1, V_DIM))
    return out.reshape(batch * nq, ATTN_WIDTH)


def _epilogue(x_new, g2, sh2, sc2, wr_t, xo_ref, h2_ref, aff_ref):
    xo_ref[...] = x_new
    h2 = _adaln_rows(x_new, g2, sh2, sc2)
    h2b = h2.astype(BF16)
    h2_ref[...] = h2b.astype(h2_ref.dtype)
    logits = lax.dot_general(wr_t, h2b, (((1,), (1,)), ((), ())),
                             preferred_element_type=F32)
    mx = jnp.max(logits, axis=0, keepdims=True)
    ex = jnp.exp(logits - mx)
    aff_ref[...] = ex / jnp.sum(ex, axis=0, keepdims=True)


def _proj_kernel(o_ref, w_ref, x_ref, gate_ref, g2_ref, sh2_ref, sc2_ref, wr_ref,
                 xo_ref, h2_ref, aff_ref):
    y = jnp.dot(o_ref[...], w_ref[...], preferred_element_type=F32)
    x_new = x_ref[...] + gate_ref[...] * y
    _epilogue(x_new, g2_ref[...], sh2_ref[...], sc2_ref[...], wr_ref[...], xo_ref, h2_ref, aff_ref)


def _epilogue_specs(d, tm, blocks_per_mod):
    mod_spec = pl.BlockSpec((None, 1, d), lambda i: (i // blocks_per_mod, 0, 0))
    in_specs = [pl.BlockSpec((1, d), lambda i: (0, 0)), mod_spec, mod_spec,
                pl.BlockSpec((N_EXPERTS, d), lambda i: (0, 0))]
    out_specs = [pl.BlockSpec((tm, d), lambda i: (i, 0)),
                 pl.BlockSpec((tm, d), lambda i: (i, 0)),
                 pl.BlockSpec((N_EXPERTS, tm), lambda i: (0, i))]
    return in_specs, out_specs


def _epilogue_out_shape(n, d):
    return [jax.ShapeDtypeStruct((n, d), F32), jax.ShapeDtypeStruct((n, d), BF16),
            jax.ShapeDtypeStruct((N_EXPERTS, n), F32)]


def _proj_mixer(o, w_bf16, x2d, gate1, g2, sh2, sc2, wr_t, tokens_per_mod):
    n, d = x2d.shape
    tm = ROW_TILE
    kdim = o.shape[1]
    bpm = tokens_per_mod // tm
    ep_in, ep_out = _epilogue_specs(d, tm, bpm)
    return pl.pallas_call(
        _proj_kernel,
        grid=(n // tm,),
        in_specs=[pl.BlockSpec((tm, kdim), lambda i: (i, 0)),
                  pl.BlockSpec((kdim, d), lambda i: (0, 0)),
                  pl.BlockSpec((tm, d), lambda i: (i, 0)),
                  pl.BlockSpec((None, 1, d), lambda i: (i // bpm, 0, 0))] + ep_in,
        out_specs=ep_out,
        out_shape=_epilogue_out_shape(n, d),
        compiler_params=_cparams(("parallel",)),
        name="proj_mixer",
    )(o, w_bf16, x2d, gate1, g2.reshape(1, d), sh2, sc2, wr_t)


def _pool_kernel(x_ref, xprev_ref, xnext_ref, g1_ref, sh1_ref, sc1_ref, pw_ref, pb_ref, ps_ref,
                 gate_ref, g2_ref, sh2_ref, sc2_ref, wr_ref, xo_ref, h2_ref, aff_ref, *, seq_len):
    tm, d = x_ref.shape
    halo = POOL_HALO
    i = pl.program_id(0)
    pos0 = (i * tm) % seq_len
    g1, sh1, sc1 = g1_ref[...], sh1_ref[...], sc1_ref[...]
    x = x_ref[...]
    h = _adaln_rows(x, g1, sh1, sc1)
    hprev = _adaln_rows(xprev_ref[...], g1, sh1, sc1)
    hnext = _adaln_rows(xnext_ref[...], g1, sh1, sc1)
    hprev = jnp.where(pos0 > 0, hprev, 0.0)
    hnext = jnp.where(pos0 + tm < seq_len, hnext, 0.0)
    hext = jnp.concatenate([hprev, h, hnext], axis=0)
    rows = tm + 2 * halo
    t = pos0 + lax.broadcasted_iota(jnp.int32, (tm, 1), 0)
    outs = []
    for gi, win in enumerate(POOL_WINDOWS):
        a = hext[:, gi * POOL_GROUP:(gi + 1) * POOL_GROUP]
        p = pltpu.roll(a, 1, axis=0) + a
        step = 1
        while 2 * step < win:
            p = pltpu.roll(p, step, axis=0) + pltpu.roll(p, rows - step, axis=0)
            step *= 2
        half = win // 2
        cnt = (jnp.minimum(t + half, seq_len) - jnp.maximum(t - half, 0)).astype(F32)
        pooled = p[halo:halo + tm, :] / cnt - a[halo:halo + tm, :]
        y = jnp.dot(pooled.astype(BF16), pw_ref[gi].astype(BF16), preferred_element_type=F32)
        outs.append(y + pb_ref[gi:gi + 1, :])
    mix = jnp.concatenate(outs, axis=1) * ps_ref[...]
    x_new = x + gate_ref[...] * mix
    _epilogue(x_new, g2_ref[...], sh2_ref[...], sc2_ref[...], wr_ref[...], xo_ref, h2_ref, aff_ref)


def _pool_mixer(x2d, g1, sh1, sc1, pool_w, pool_b, pool_scale, gate1, g2, sh2, sc2, wr_t,
                tokens_per_mod, seq_len):
    n, d = x2d.shape
    tm = min(ROW_TILE, seq_len)
    bpm = tokens_per_mod // tm
    hb = tm // POOL_HALO
    n_hblocks = n // POOL_HALO
    mod_spec = pl.BlockSpec((None, 1, d), lambda i: (i // bpm, 0, 0))
    ep_in, ep_out = _epilogue_specs(d, tm, bpm)
    return pl.pallas_call(
        functools.partial(_pool_kernel, seq_len=seq_len),
        grid=(n // tm,),
        in_specs=[pl.BlockSpec((tm, d), lambda i: (i, 0)),
                  pl.BlockSpec((POOL_HALO, d), lambda i: (jnp.maximum(i * hb - 1, 0), 0)),
                  pl.BlockSpec((POOL_HALO, d),
                               lambda i: (jnp.minimum((i + 1) * hb, n_hblocks - 1), 0)),
                  pl.BlockSpec((1, d), lambda i: (0, 0)), mod_spec, mod_spec,
                  pl.BlockSpec((N_POOL_GROUPS, POOL_GROUP, POOL_GROUP), lambda i: (0, 0, 0)),
                  pl.BlockSpec((N_POOL_GROUPS, POOL_GROUP), lambda i: (0, 0)),
                  pl.BlockSpec((1, d), lambda i: (0, 0)),
                  mod_spec] + ep_in,
        out_specs=ep_out,
        out_shape=_epilogue_out_shape(n, d),
        compiler_params=_cparams(("parallel",)),
        name="pool_mixer",
    )(x2d, x2d, x2d, g1.reshape(1, d), sh1, sc1, pool_w, pool_b, pool_scale.reshape(1, d),
      gate1, g2.reshape(1, d), sh2, sc2, wr_t)


def _ffn_kernel(xe_ref, wg_ref, wu_ref, wd_ref, g_ref, ye_ref, *, rows):
    def body(i, carry):
        r0 = pl.multiple_of(i * rows, rows)
        x = xe_ref[pl.ds(r0, rows), :]
        a = jnp.dot(x, wg_ref[...], preferred_element_type=F32)
        b = jnp.dot(x, wu_ref[...], preferred_element_type=F32)
        hid = (a * (1.0 / (1.0 + jnp.exp(-a))) * b).astype(BF16)
        y = jnp.dot(hid, wd_ref[...], preferred_element_type=F32)
        ye_ref[pl.ds(r0, rows), :] = y * g_ref[pl.ds(r0, rows), :]
        return carry

    lax.fori_loop(0, xe_ref.shape[0] // rows, body, 0)


def _expert_ffn(xe, wg, wu, wd, gates):
    s, e, c, d = xe.shape
    f = wg.shape[2]
    return pl.pallas_call(
        functools.partial(_ffn_kernel, rows=min(FFN_ROWS, c)),
        grid=(e, s),
        in_specs=[pl.BlockSpec((None, None, c, d), lambda ei, si: (si, ei, 0, 0)),
                  pl.BlockSpec((None, d, f), lambda ei, si: (ei, 0, 0)),
                  pl.BlockSpec((None, d, f), lambda ei, si: (ei, 0, 0)),
                  pl.BlockSpec((None, f, d), lambda ei, si: (ei, 0, 0)),
                  pl.BlockSpec((None, None, c, 1), lambda ei, si: (si, ei, 0, 0))],
        out_specs=pl.BlockSpec((None, None, c, d), lambda ei, si: (si, ei, 0, 0)),
        out_shape=jax.ShapeDtypeStruct((s, e, c, d), F32),
        compiler_params=_cparams(("parallel", "arbitrary")),
        name="expert_ffn",
    )(xe, wg, wu, wd, gates)


def _final_norm_kernel(x_ref, f_ref, gate_ref, g_ref, o_ref):
    x = x_ref[...] + gate_ref[...] * f_ref[...]
    ms = jnp.mean(x * x, axis=-1, keepdims=True)
    o_ref[...] = x * lax.rsqrt(ms + EPS) * g_ref[...]


def _final_norm(x, f, gate2, g, tokens_per_mod):
    n, d = x.shape
    tm = ROW_TILE
    bpm = tokens_per_mod // tm
    return pl.pallas_call(
        _final_norm_kernel,
        grid=(n // tm,),
        in_specs=[pl.BlockSpec((tm, d), lambda i: (i, 0)),
                  pl.BlockSpec((tm, d), lambda i: (i, 0)),
                  pl.BlockSpec((None, 1, d), lambda i: (i // bpm, 0, 0)),
                  pl.BlockSpec((1, d), lambda i: (0, 0))],
        out_specs=pl.BlockSpec((tm, d), lambda i: (i, 0)),
        out_shape=jax.ShapeDtypeStruct((n, d), F32),
        compiler_params=_cparams(("parallel",)),
        name="final_norm",
    )(x, f, gate2, g.reshape(1, d))


def _rope_tables(n_tokens):
    t = jnp.arange(n_tokens)
    row = (t // GRID_W).astype(F32)
    col = (t % GRID_W).astype(F32)
    half = HEAD_DIM // 2
    inv = ROPE_BASE ** (-jnp.arange(0, half, 2, dtype=F32) / half)
    ar, ac = row[:, None] * inv, col[:, None] * inv
    c64 = jnp.concatenate([jnp.cos(ar), jnp.cos(ar), jnp.cos(ac), jnp.cos(ac)], axis=-1)
    s64 = jnp.concatenate([-jnp.sin(ar), jnp.sin(ar), -jnp.sin(ac), jnp.sin(ac)], axis=-1)
    return jnp.tile(c64, (1, 2)), jnp.tile(s64, (1, 2))


def _route_and_ffn(h2_list, aff_list, wg, wu, wd):
    caps = [EC_CAPACITY * h.shape[0] // N_EXPERTS for h in h2_list]
    assert caps[0] == caps[1]
    gs, idxs, xes = [], [], []
    for h2, aff_t in zip(h2_list, aff_list):
        g, idx = lax.top_k(aff_t, caps[0])
        gs.append(g)
        idxs.append(idx)
        xes.append(h2[idx])
    ye = _expert_ffn(jnp.stack(xes), wg, wu, wd, jnp.stack(gs)[..., None])
    outs = []
    for s, h2 in enumerate(h2_list):
        n, d = h2.shape
        outs.append(jnp.zeros((n, d), F32).at[idxs[s].reshape(-1)].add(ye[s].reshape(-1, d)))
    return outs


def kernel(x_prompt, x_sample, cache_k, cache_v, c, c_ctx, w_mod, b_mod, norm1_g, norm2_g,
           attn_wqkv, attn_wo, lambda_q1, lambda_k1, lambda_q2, lambda_k2, subln_g,
           pool_w, pool_b, pool_scale, router_w, expert_w_gate, expert_w_up, expert_w_down,
           final_g):
    bp, n_ctx, d = x_prompt.shape
    bs, n_lat, _ = x_sample.shape
    past = cache_k.shape[2]
    n_p, n_s = bp * n_ctx, bs * n_lat

    rows = 8 * ((1 + bs + 7) // 8)
    cvecs = jnp.zeros((rows, d), F32).at[0].set(c_ctx).at[1:1 + bs].set(c)
    mod = _modulation_all(cvecs, w_mod, b_mod)
    rope_tabs = _rope_tables(n_lat)

    xp = x_prompt.reshape(n_p, d)
    xs = x_sample.reshape(n_s, d)
    fp = fs = None
    gp2 = gs2 = None
    new_k, new_v = [], []
    for l in range(DEPTH):
        if l > 0:
            xp = (xp.reshape(1, n_p, d) + gp2 * fp.reshape(1, n_p, d)).reshape(n_p, d)
            xs = (xs.reshape(bs, n_lat, d) + gs2 * fs.reshape(bs, n_lat, d)).reshape(n_s, d)
        mp_ = mod[l, 0:1].reshape(1, 6, 1, d)
        ms_ = mod[l, 1:1 + bs].reshape(bs, 6, 1, d)
        sp1, cp1, gp1, sp2, cp2, gp2 = [mp_[:, i] for i in range(6)]
        ss1, cs1, gs1, ss2, cs2, gs2 = [ms_[:, i] for i in range(6)]
        wr_t = router_w[l].T.astype(BF16)
        if l % 2 == 0:
            a = l // 2
            lam_init = 0.8 - 0.6 * math.exp(-0.3 * l)
            lamv = jnp.stack([lambda_q1[a], lambda_k1[a], lambda_q2[a], lambda_k2[a]])
            wqkv = attn_wqkv[a].astype(BF16)
            wo = attn_wo[a].astype(BF16)
            qp, kp, vp = _qkv_project(xp, norm1_g[l], sp1, cp1, wqkv, None, n_p, F32)
            new_k.append(kp.reshape(bp, n_ctx, N_HEADS, 2 * HEAD_DIM))
            new_v.append(vp.reshape(bp, n_ctx, N_HEADS, V_DIM))
            op = _diff_attention(qp, kp, vp, lamv, subln_g[a], lam_init, bp)
            xp, h2p, affp = _proj_mixer(op, wo, xp, gp1, norm2_g[l], sp2, cp2, wr_t, n_p)
            qs, ks_, vs = _qkv_project(xs, norm1_g[l], ss1, cs1, wqkv, rope_tabs, n_lat, BF16)
            k_all = jnp.concatenate([ks_.reshape(bs, n_lat, ATTN_WIDTH),
                                     cache_k[:, a].reshape(bs, past, ATTN_WIDTH).astype(BF16)],
                                    axis=1).reshape(bs * (n_lat + past), ATTN_WIDTH)
            v_all = jnp.concatenate([vs.reshape(bs, n_lat, ATTN_WIDTH),
                                     cache_v[:, a].reshape(bs, past, ATTN_WIDTH).astype(BF16)],
                                    axis=1).reshape(bs * (n_lat + past), ATTN_WIDTH)
            os_ = _diff_attention(qs, k_all, v_all, lamv, subln_g[a], lam_init, bs)
            xs, h2s, affs = _proj_mixer(os_, wo, xs, gs1, norm2_g[l], ss2, cs2, wr_t, n_lat)
        else:
            p = l // 2
            xp, h2p, affp = _pool_mixer(xp, norm1_g[l], sp1, cp1, pool_w[p], pool_b[p],
                                        pool_scale[p], gp1, norm2_g[l], sp2, cp2, wr_t,
                                        n_p, n_ctx)
            xs, h2s, affs = _pool_mixer(xs, norm1_g[l], ss1, cs1, pool_w[p], pool_b[p],
                                        pool_scale[p], gs1, norm2_g[l], ss2, cs2, wr_t,
                                        n_lat, n_lat)
        fp, fs = _route_and_ffn([h2p, h2s], [affp, affs], expert_w_gate[l].astype(BF16),
                                expert_w_up[l].astype(BF16), expert_w_down[l].astype(BF16))
    y_prompt = _final_norm(xp, fp, gp2, final_g, n_p).reshape(bp, n_ctx, d)
    y_sample = _final_norm(xs, fs, gs2, final_g, n_lat).reshape(bs, n_lat, d)
    return (y_prompt, y_sample, jnp.stack(new_k, axis=1), jnp.stack(new_v, axis=1))
```

```python
import math
import functools
import jax
import jax.numpy as jnp
from jax import lax
from jax.experimental import pallas as pl
from jax.experimental.pallas import tpu as pltpu

D_MODEL = 1024
DEPTH = 4
GRID_W = 64
N_HEADS = 8
HEAD_DIM = 64
V_DIM = 2 * HEAD_DIM
ATTN_WIDTH = N_HEADS * V_DIM
ROPE_BASE = 10000.0
N_POOL_GROUPS = 4
POOL_GROUP = D_MODEL // N_POOL_GROUPS
POOL_WINDOWS = (2, 4, 8, 16)
POOL_HALO = 8
N_EXPERTS = 16
EC_CAPACITY = 2
EPS = 1e-6

F32 = jnp.float32
BF16 = jnp.bfloat16
LANES = 128
LOG2E = 1.4426950408889634
VMEM_LIMIT = 56 * 1024 * 1024
ROW_TILE = 512
ATTN_TQ = 256
ATTN_TK = 1536
ATTN_HEADS_PER_STEP = 2
FFN_ROWS = 512
GATHER_UNROLL = 8
COMBINE_TM = 256
COMBINE_CH = 32
COMBINE_GROUP = 256


def _cparams(sem):
    return pltpu.CompilerParams(dimension_semantics=sem, vmem_limit_bytes=VMEM_LIMIT)


def _mod_kernel(c_ref, w_ref, b_ref, o_ref):
    c = c_ref[...]
    s = (c * (1.0 / (1.0 + jnp.exp(-c)))).astype(BF16)
    o_ref[...] = jnp.dot(s, w_ref[...].astype(BF16), preferred_element_type=F32) + b_ref[...]


def _modulation_all(cvecs, w_mod, b_mod):
    r, d = cvecs.shape
    depth, _, n6 = w_mod.shape
    tn = 1024
    return pl.pallas_call(
        _mod_kernel,
        grid=(depth, n6 // tn),
        in_specs=[pl.BlockSpec((r, d), lambda l, j: (0, 0)),
                  pl.BlockSpec((None, d, tn), lambda l, j: (l, 0, j)),
                  pl.BlockSpec((None, 1, tn), lambda l, j: (l, 0, j))],
        out_specs=pl.BlockSpec((None, r, tn), lambda l, j: (l, 0, j)),
        out_shape=jax.ShapeDtypeStruct((depth, r, n6), F32),
        compiler_params=_cparams(("parallel", "parallel")),
        name="modulation",
    )(cvecs, w_mod, b_mod.reshape(depth, 1, n6))


def _adaln_rows(x, g, shift, scale):
    ms = jnp.mean(x * x, axis=-1, keepdims=True)
    return (x * lax.rsqrt(ms + EPS)) * g * (1.0 + scale) + shift


def _swap16(x):
    lane = lax.broadcasted_iota(jnp.int32, x.shape, 1)
    up = pltpu.roll(x, x.shape[1] - 16, axis=1)
    dn = pltpu.roll(x, 16, axis=1)
    return jnp.where((lane % 32) < 16, up, dn)


def _qkv_kernel(*refs, rope):
    if rope:
        x_ref, g_ref, sh_ref, sc_ref, w_ref, cos_ref, sin_ref, q_ref, k_ref, v_ref = refs
    else:
        x_ref, g_ref, sh_ref, sc_ref, w_ref, q_ref, k_ref, v_ref = refs
    h = _adaln_rows(x_ref[...], g_ref[...], sh_ref[...], sc_ref[...]).astype(BF16)
    width = q_ref.shape[1]
    qscale = (HEAD_DIM ** -0.5) * LOG2E
    for part, o_ref in enumerate((q_ref, k_ref, v_ref)):
        r = jnp.dot(h, w_ref[:, part * width:(part + 1) * width], preferred_element_type=F32)
        if part < 2 and rope:
            c = cos_ref[...]
            s = sin_ref[...]
            for hd in range(width // LANES):
                blk = r[:, hd * LANES:(hd + 1) * LANES]
                blk = blk * c + _swap16(blk) * s
                if part == 0:
                    blk = blk * qscale
                o_ref[:, hd * LANES:(hd + 1) * LANES] = blk.astype(o_ref.dtype)
        else:
            if part == 0:
                r = r * qscale
            o_ref[...] = r.astype(o_ref.dtype)


def _qkv_project(x2d, g, shift, scale, w_bf16, rope_tabs, tokens_per_mod, kv_dtype):
    n, d = x2d.shape
    tm = ROW_TILE
    width = w_bf16.shape[1] // 3
    rope = rope_tabs is not None
    blocks_per_mod = tokens_per_mod // tm
    in_specs = [pl.BlockSpec((tm, d), lambda i: (i, 0)),
                pl.BlockSpec((1, d), lambda i: (0, 0)),
                pl.BlockSpec((None, 1, d), lambda i: (i // blocks_per_mod, 0, 0)),
                pl.BlockSpec((None, 1, d), lambda i: (i // blocks_per_mod, 0, 0)),
                pl.BlockSpec((d, 3 * width), lambda i: (0, 0))]
    args = [x2d, g.reshape(1, d), shift, scale, w_bf16]
    if rope:
        cos_t, sin_t = rope_tabs
        seq_blocks = cos_t.shape[0] // tm
        in_specs += [pl.BlockSpec((tm, LANES), lambda i: (i % seq_blocks, 0)),
                     pl.BlockSpec((tm, LANES), lambda i: (i % seq_blocks, 0))]
        args += [cos_t, sin_t]
    out_spec = pl.BlockSpec((tm, width), lambda i: (i, 0))
    return pl.pallas_call(
        functools.partial(_qkv_kernel, rope=rope),
        grid=(n // tm,),
        in_specs=in_specs,
        out_specs=[out_spec, out_spec, out_spec],
        out_shape=[jax.ShapeDtypeStruct((n, width), BF16),
                   jax.ShapeDtypeStruct((n, width), kv_dtype),
                   jax.ShapeDtypeStruct((n, width), kv_dtype)],
        compiler_params=_cparams(("parallel",)),
        name="adaln_qkv_rope" if rope else "adaln_qkv",
    )(*args)


def _attn_kernel(lamv_ref, q_ref, k_ref, v_ref, g_ref, o_ref, qs_ref, m_ref, l_ref, acc_ref,
                 *, tk, lam_init, hp):
    tq = q_ref.shape[0]
    nk = k_ref.shape[0]
    for h in range(hp):
        q = q_ref[:, h * LANES:(h + 1) * LANES]
        lane = lax.broadcasted_iota(jnp.int32, q.shape, 1)
        zero = jnp.zeros_like(q)
        qs_ref[h, 0:tq, :] = jnp.where(lane < HEAD_DIM, q, zero)
        qs_ref[h, tq:2 * tq, :] = jnp.where(lane >= HEAD_DIM, q, zero)
    m_ref[...] = jnp.full(m_ref.shape, -jnp.inf, F32)
    l_ref[...] = jnp.zeros(l_ref.shape, F32)
    acc_ref[...] = jnp.zeros(acc_ref.shape, F32)

    def body(j, carry):
        off = pl.multiple_of(j * tk, tk)
        for h in range(hp):
            kc = k_ref[pl.ds(off, tk), h * LANES:(h + 1) * LANES].astype(BF16)
            vc = v_ref[pl.ds(off, tk), h * LANES:(h + 1) * LANES].astype(BF16)
            s = lax.dot_general(qs_ref[h], kc, (((1,), (1,)), ((), ())),
                                preferred_element_type=F32)
            m_old = m_ref[h]
            m_new = jnp.maximum(m_old, jnp.max(s, axis=1)[:, None])
            alpha = jnp.exp2(m_old - m_new)
            p = jnp.exp2(s - jnp.tile(m_new, (1, tk // LANES)))
            l_ref[h] = alpha * l_ref[h] + jnp.sum(p, axis=1)[:, None]
            acc_ref[h] = alpha * acc_ref[h] + jnp.dot(p.astype(BF16), vc,
                                                      preferred_element_type=F32)
            m_ref[h] = m_new
        return carry

    lax.fori_loop(0, nk // tk, body, 0)

    lv = lamv_ref[...]
    lam = (jnp.exp(jnp.sum(lv[0:1, :] * lv[1:2, :], axis=-1, keepdims=True))
           - jnp.exp(jnp.sum(lv[2:3, :] * lv[3:4, :], axis=-1, keepdims=True)) + lam_init)
    for h in range(hp):
        o = acc_ref[h] / l_ref[h]
        d = o[0:tq, :] - lam * o[tq:2 * tq, :]
        ms = jnp.mean(d * d, axis=-1, keepdims=True)
        o_ref[:, h * LANES:(h + 1) * LANES] = (
            (d * lax.rsqrt(ms + EPS)) * g_ref[...] * (1.0 - lam_init)).astype(o_ref.dtype)


def _diff_attention(q, k, v, lamv, subln_g, lam_init, batch):
    nq = q.shape[0] // batch
    nk = k.shape[0] // batch
    tq = min(ATTN_TQ, nq)
    tk = ATTN_TK if nk % ATTN_TK == 0 else nk
    assert nq % tq == 0 and nk % tk == 0 and tk % LANES == 0
    hp = ATTN_HEADS_PER_STEP
    w = hp * LANES
    q3 = q.reshape(batch, nq, ATTN_WIDTH)
    k3 = k.reshape(batch, nk, ATTN_WIDTH)
    v3 = v.reshape(batch, nk, ATTN_WIDTH)
    out = pl.pallas_call(
        functools.partial(_attn_kernel, tk=tk, lam_init=lam_init, hp=hp),
        grid=(batch, N_HEADS // hp, nq // tq),
        in_specs=[pl.BlockSpec((4, HEAD_DIM), lambda b, h, i: (0, 0)),
                  pl.BlockSpec((None, tq, w), lambda b, h, i: (b, i, h)),
                  pl.BlockSpec((None, nk, w), lambda b, h, i: (b, 0, h)),
                  pl.BlockSpec((None, nk, w), lambda b, h, i: (b, 0, h)),
                  pl.BlockSpec((1, V_DIM), lambda b, h, i: (0, 0))],
        out_specs=pl.BlockSpec((None, tq, w), lambda b, h, i: (b, i, h)),
        out_shape=jax.ShapeDtypeStruct((batch, nq, ATTN_WIDTH), BF16),
        scratch_shapes=[pltpu.VMEM((hp, 2 * tq, V_DIM), BF16),
                        pltpu.VMEM((hp, 2 * tq, LANES), F32),
                        pltpu.VMEM((hp, 2 * tq, LANES), F32),
                        pltpu.VMEM((hp, 2 * tq, V_DIM), F32)],
        compiler_params=_cparams(("parallel", "parallel", "parallel")),
        name="diff_attention",
    )(lamv, q3, k3, v3, subln_g.reshape(1, V_DIM))
    return out.reshape(batch * nq, ATTN_WIDTH)


def _epilogue(x_new, g2, sh2, sc2, wr_t, xo_ref, h2_ref, aff_ref):
    xo_ref[...] = x_new
    h2 = _adaln_rows(x_new, g2, sh2, sc2)
    h2b = h2.astype(BF16)
    for j in range(h2_ref.shape[1]):
        h2_ref[:, j, :] = h2[:, j * LANES:(j + 1) * LANES]
    logits = lax.dot_general(wr_t, h2b, (((1,), (1,)), ((), ())),
                             preferred_element_type=F32)
    mx = jnp.max(logits, axis=0, keepdims=True)
    ex = jnp.exp(logits - mx)
    aff_ref[...] = ex / jnp.sum(ex, axis=0, keepdims=True)


def _proj_kernel(o_ref, w_ref, x_ref, gate_ref, g2_ref, sh2_ref, sc2_ref, wr_ref,
                 xo_ref, h2_ref, aff_ref):
    y = jnp.dot(o_ref[...], w_ref[...], preferred_element_type=F32)
    x_new = x_ref[...] + gate_ref[...] * y
    _epilogue(x_new, g2_ref[...], sh2_ref[...], sc2_ref[...], wr_ref[...], xo_ref, h2_ref, aff_ref)


def _epilogue_specs(d, tm, blocks_per_mod):
    mod_spec = pl.BlockSpec((None, 1, d), lambda i: (i // blocks_per_mod, 0, 0))
    in_specs = [pl.BlockSpec((1, d), lambda i: (0, 0)), mod_spec, mod_spec,
                pl.BlockSpec((N_EXPERTS, d), lambda i: (0, 0))]
    out_specs = [pl.BlockSpec((tm, d), lambda i: (i, 0)),
                 pl.BlockSpec((tm, d // LANES, LANES), lambda i: (i, 0, 0)),
                 pl.BlockSpec((N_EXPERTS, tm), lambda i: (0, i))]
    return in_specs, out_specs


def _epilogue_out_shape(n, d):
    return [jax.ShapeDtypeStruct((n, d), F32),
            jax.ShapeDtypeStruct((n, d // LANES, LANES), F32),
            jax.ShapeDtypeStruct((N_EXPERTS, n), F32)]


def _proj_mixer(o, w_bf16, x2d, gate1, g2, sh2, sc2, wr_t, tokens_per_mod):
    n, d = x2d.shape
    tm = ROW_TILE
    kdim = o.shape[1]
    bpm = tokens_per_mod // tm
    ep_in, ep_out = _epilogue_specs(d, tm, bpm)
    return pl.pallas_call(
        _proj_kernel,
        grid=(n // tm,),
        in_specs=[pl.BlockSpec((tm, kdim), lambda i: (i, 0)),
                  pl.BlockSpec((kdim, d), lambda i: (0, 0)),
                  pl.BlockSpec((tm, d), lambda i: (i, 0)),
                  pl.BlockSpec((None, 1, d), lambda i: (i // bpm, 0, 0))] + ep_in,
        out_specs=ep_out,
        out_shape=_epilogue_out_shape(n, d),
        compiler_params=_cparams(("parallel",)),
        name="proj_mixer",
    )(o, w_bf16, x2d, gate1, g2.reshape(1, d), sh2, sc2, wr_t)


def _pool_kernel(x_ref, xprev_ref, xnext_ref, g1_ref, sh1_ref, sc1_ref, pw_ref, pb_ref, ps_ref,
                 gate_ref, g2_ref, sh2_ref, sc2_ref, wr_ref, xo_ref, h2_ref, aff_ref, *, seq_len):
    tm, d = x_ref.shape
    halo = POOL_HALO
    i = pl.program_id(0)
    pos0 = (i * tm) % seq_len
    g1, sh1, sc1 = g1_ref[...], sh1_ref[...], sc1_ref[...]
    x = x_ref[...]
    h = _adaln_rows(x, g1, sh1, sc1)
    hprev = _adaln_rows(xprev_ref[...], g1, sh1, sc1)
    hnext = _adaln_rows(xnext_ref[...], g1, sh1, sc1)
    hprev = jnp.where(pos0 > 0, hprev, 0.0)
    hnext = jnp.where(pos0 + tm < seq_len, hnext, 0.0)
    hext = jnp.concatenate([hprev, h, hnext], axis=0)
    rows = tm + 2 * halo
    t = pos0 + lax.broadcasted_iota(jnp.int32, (tm, 1), 0)
    outs = []
    for gi, win in enumerate(POOL_WINDOWS):
        a = hext[:, gi * POOL_GROUP:(gi + 1) * POOL_GROUP]
        p = pltpu.roll(a, 1, axis=0) + a
        step = 1
        while 2 * step < win:
            p = pltpu.roll(p, step, axis=0) + pltpu.roll(p, rows - step, axis=0)
            step *= 2
        half = win // 2
        cnt = (jnp.minimum(t + half, seq_len) - jnp.maximum(t - half, 0)).astype(F32)
        pooled = p[halo:halo + tm, :] / cnt - a[halo:halo + tm, :]
        y = jnp.dot(pooled.astype(BF16), pw_ref[gi].astype(BF16), preferred_element_type=F32)
        outs.append(y + pb_ref[gi:gi + 1, :])
    mix = jnp.concatenate(outs, axis=1) * ps_ref[...]
    x_new = x + gate_ref[...] * mix
    _epilogue(x_new, g2_ref[...], sh2_ref[...], sc2_ref[...], wr_ref[...], xo_ref, h2_ref, aff_ref)


def _pool_mixer(x2d, g1, sh1, sc1, pool_w, pool_b, pool_scale, gate1, g2, sh2, sc2, wr_t,
                tokens_per_mod, seq_len):
    n, d = x2d.shape
    tm = min(ROW_TILE, seq_len)
    bpm = tokens_per_mod // tm
    hb = tm // POOL_HALO
    n_hblocks = n // POOL_HALO
    mod_spec = pl.BlockSpec((None, 1, d), lambda i: (i // bpm, 0, 0))
    ep_in, ep_out = _epilogue_specs(d, tm, bpm)
    return pl.pallas_call(
        functools.partial(_pool_kernel, seq_len=seq_len),
        grid=(n // tm,),
        in_specs=[pl.BlockSpec((tm, d), lambda i: (i, 0)),
                  pl.BlockSpec((POOL_HALO, d), lambda i: (jnp.maximum(i * hb - 1, 0), 0)),
                  pl.BlockSpec((POOL_HALO, d),
                               lambda i: (jnp.minimum((i + 1) * hb, n_hblocks - 1), 0)),
                  pl.BlockSpec((1, d), lambda i: (0, 0)), mod_spec, mod_spec,
                  pl.BlockSpec((N_POOL_GROUPS, POOL_GROUP, POOL_GROUP), lambda i: (0, 0, 0)),
                  pl.BlockSpec((N_POOL_GROUPS, POOL_GROUP), lambda i: (0, 0)),
                  pl.BlockSpec((1, d), lambda i: (0, 0)),
                  mod_spec] + ep_in,
        out_specs=ep_out,
        out_shape=_epilogue_out_shape(n, d),
        compiler_params=_cparams(("parallel",)),
        name="pool_mixer",
    )(x2d, x2d, x2d, g1.reshape(1, d), sh1, sc1, pool_w, pool_b, pool_scale.reshape(1, d),
      gate1, g2.reshape(1, d), sh2, sc2, wr_t)


def _ffn_gather_copy(h2p_hbm, xbuf, sem, slot, src_row, dst_row, nrows):
    return pltpu.make_async_copy(h2p_hbm.at[pl.ds(src_row, nrows)],
                                 xbuf.at[slot, pl.ds(dst_row, nrows)], sem.at[slot])


def _ffn_kernel(idx_ref, h2p_hbm, wg_ref, wu_ref, wd_ref, g_ref, ye_ref, xbuf, sem,
                *, rows, n_tokens):
    n_e, n_s = pl.num_programs(0), pl.num_programs(1)
    ei, si = pl.program_id(0), pl.program_id(1)
    cap, pieces = xbuf.shape[1], xbuf.shape[2]
    step = ei * n_s + si
    slot = step % 2

    def issue(e_t, s_t, slot_t):
        base = (s_t * n_e + e_t) * cap
        row0 = s_t * n_tokens

        def body(kb, carry):
            k0 = kb * GATHER_UNROLL
            for u in range(GATHER_UNROLL):
                _ffn_gather_copy(h2p_hbm, xbuf, sem, slot_t, row0 + idx_ref[base + k0 + u],
                                 k0 + u, 1).start()
            return carry

        lax.fori_loop(0, cap // GATHER_UNROLL, body, 0)

    @pl.when(step == 0)
    def _():
        issue(ei, si, slot)

    @pl.when(step + 1 < n_e * n_s)
    def _():
        nxt = step + 1
        issue(nxt // n_s, nxt % n_s, 1 - slot)

    _ffn_gather_copy(h2p_hbm, xbuf, sem, slot, 0, 0, cap).wait()

    def body(i, carry):
        r0 = pl.multiple_of(i * rows, rows)
        x = jnp.concatenate([xbuf[slot, pl.ds(r0, rows), j, :].astype(BF16)
                             for j in range(pieces)], axis=1)
        a = jnp.dot(x, wg_ref[...], preferred_element_type=F32)
        b = jnp.dot(x, wu_ref[...], preferred_element_type=F32)
        hid = (a * (1.0 / (1.0 + jnp.exp(-a))) * b).astype(BF16)
        y = jnp.dot(hid, wd_ref[...], preferred_element_type=F32)
        ye_ref[pl.ds(r0, rows), :] = (y * g_ref[pl.ds(r0, rows), :]).astype(ye_ref.dtype)
        return carry

    lax.fori_loop(0, cap // rows, body, 0)


def _expert_ffn(idx_flat, h2p_all, wg, wu, wd, gates, n_tokens):
    s, e, c, _ = gates.shape
    d, f = wg.shape[1], wg.shape[2]
    return pl.pallas_call(
        functools.partial(_ffn_kernel, rows=min(FFN_ROWS, c), n_tokens=n_tokens),
        grid_spec=pltpu.PrefetchScalarGridSpec(
            num_scalar_prefetch=1,
            grid=(e, s),
            in_specs=[pl.BlockSpec(memory_space=pl.ANY),
                      pl.BlockSpec((None, d, f), lambda ei, si, idx: (ei, 0, 0)),
                      pl.BlockSpec((None, d, f), lambda ei, si, idx: (ei, 0, 0)),
                      pl.BlockSpec((None, f, d), lambda ei, si, idx: (ei, 0, 0)),
                      pl.BlockSpec((None, None, c, 1), lambda ei, si, idx: (si, ei, 0, 0))],
            out_specs=pl.BlockSpec((None, None, c, d), lambda ei, si, idx: (si, ei, 0, 0)),
            scratch_shapes=[pltpu.VMEM((2, c, d // LANES, LANES), F32),
                            pltpu.SemaphoreType.DMA((2,))]),
        out_shape=jax.ShapeDtypeStruct((s, e, c, d), BF16),
        compiler_params=pltpu.CompilerParams(dimension_semantics=("arbitrary", "arbitrary"),
                                             vmem_limit_bytes=VMEM_LIMIT,
                                             disable_bounds_checks=True),
        name="expert_ffn",
    )(idx_flat, h2p_all, wg, wu, wd, gates)


def _combine_kernel(clo_ref, ncnt_ref, ye_hbm, tok_hbm, x_ref, gate_ref, fg_ref, o_ref,
                    zbuf, tokbuf, acc_ref, sem, *, final):
    n_e = ye_hbm.shape[0]
    tm, d = x_ref.shape
    ch, grp = COMBINE_CH, COMBINE_GROUP
    i = pl.program_id(0)
    nt = pl.num_programs(0)
    slot = i % 2

    def chunk_copies(e, src_row, slot_t, dst_row):
        return (pltpu.make_async_copy(ye_hbm.at[e, pl.ds(src_row, ch), :],
                                      zbuf.at[slot_t, pl.ds(dst_row, ch), :], sem.at[slot_t]),
                pltpu.make_async_copy(tok_hbm.at[e, pl.ds(src_row, ch), :],
                                      tokbuf.at[slot_t, pl.ds(dst_row, ch), :], sem.at[slot_t]))

    def issue(tile, slot_t):
        pos = 0
        for e in range(n_e):
            lo = clo_ref[tile * n_e + e]

            def body(j, p, e=e, lo=lo):
                src = pl.multiple_of((lo + j) * ch, ch)
                dst = pl.multiple_of(p * ch, ch)
                for cp in chunk_copies(e, src, slot_t, dst):
                    cp.start()
                return p + 1

            pos = lax.fori_loop(0, ncnt_ref[tile * n_e + e], body, pos)

    def total_chunks(tile):
        tot = 0
        for e in range(n_e):
            tot = tot + ncnt_ref[tile * n_e + e]
        return tot

    @pl.when(i == 0)
    def _():
        zbuf[...] = jnp.zeros(zbuf.shape, zbuf.dtype)
        tokbuf[...] = jnp.full(tokbuf.shape, -1, jnp.int32)
        issue(i, slot)

    @pl.when(i + 1 < nt)
    def _():
        issue(i + 1, 1 - slot)

    total = total_chunks(i)

    def wait_body(j, carry):
        for cp in chunk_copies(0, 0, slot, 0):
            cp.wait()
        return carry

    lax.fori_loop(0, total, wait_body, 0)

    acc_ref[...] = jnp.zeros(acc_ref.shape, F32)
    tok_of_lane = i * tm + lax.broadcasted_iota(jnp.int32, (grp, tm), 1)
    row_in_group = lax.broadcasted_iota(jnp.int32, (grp, 1), 0)

    def group_body(gidx, carry):
        r0 = pl.multiple_of(gidx * grp, grp)
        tok = jnp.where(r0 + row_in_group < total * ch, tokbuf[slot, pl.ds(r0, grp), :], -1)
        onehot_t = (tok == tok_of_lane).astype(BF16)
        acc_ref[...] += lax.dot_general(onehot_t, zbuf[slot, pl.ds(r0, grp), :],
                                        (((0,), (0,)), ((), ())), preferred_element_type=F32)
        return carry

    lax.fori_loop(0, (total * ch + grp - 1) // grp, group_body, 0)

    x = x_ref[...] + gate_ref[...] * acc_ref[...]
    if final:
        ms = jnp.mean(x * x, axis=-1, keepdims=True)
        x = x * lax.rsqrt(ms + EPS) * fg_ref[...]
    o_ref[...] = x


def _combine(ye, tok_col, idx_sorted, x2d, gate2, final_g, tokens_per_mod, final):
    n, d = x2d.shape
    e, c, _ = ye.shape
    tm, ch = COMBINE_TM, COMBINE_CH
    n_tiles = n // tm
    bpm = tokens_per_mod // tm
    bounds = jnp.arange(n_tiles + 1, dtype=jnp.int32) * tm
    first = jnp.sum(idx_sorted[:, :, None] < bounds[None, None, :], axis=1).astype(jnp.int32)
    clo = first[:, :-1] // ch
    chi = (first[:, 1:] + ch - 1) // ch
    ncnt = jnp.maximum(chi - clo, 0)
    max_rows = e * (tm // ch + 2) * ch
    max_rows = ((max_rows + COMBINE_GROUP - 1) // COMBINE_GROUP) * COMBINE_GROUP
    return pl.pallas_call(
        functools.partial(_combine_kernel, final=final),
        grid_spec=pltpu.PrefetchScalarGridSpec(
            num_scalar_prefetch=2,
            grid=(n_tiles,),
            in_specs=[pl.BlockSpec(memory_space=pl.ANY),
                      pl.BlockSpec(memory_space=pl.ANY),
                      pl.BlockSpec((tm, d), lambda i, a, b: (i, 0)),
                      pl.BlockSpec((None, 1, d), lambda i, a, b: (i // bpm, 0, 0)),
                      pl.BlockSpec((1, d), lambda i, a, b: (0, 0))],
            out_specs=pl.BlockSpec((tm, d), lambda i, a, b: (i, 0)),
            scratch_shapes=[pltpu.VMEM((2, max_rows, d), BF16),
                            pltpu.VMEM((2, max_rows, 1), jnp.int32),
                            pltpu.VMEM((tm, d), F32),
                            pltpu.SemaphoreType.DMA((2,))]),
        out_shape=jax.ShapeDtypeStruct((n, d), F32),
        compiler_params=_cparams(("arbitrary",)),
        name="combine_final" if final else "combine",
    )(clo.T.reshape(-1), ncnt.T.reshape(-1), ye, tok_col, x2d, gate2, final_g.reshape(1, d))


def _rope_tables(n_tokens):
    t = jnp.arange(n_tokens)
    row = (t // GRID_W).astype(F32)
    col = (t % GRID_W).astype(F32)
    half = HEAD_DIM // 2
    inv = ROPE_BASE ** (-jnp.arange(0, half, 2, dtype=F32) / half)
    ar, ac = row[:, None] * inv, col[:, None] * inv
    c64 = jnp.concatenate([jnp.cos(ar), jnp.cos(ar), jnp.cos(ac), jnp.cos(ac)], axis=-1)
    s64 = jnp.concatenate([-jnp.sin(ar), jnp.sin(ar), -jnp.sin(ac), jnp.sin(ac)], axis=-1)
    return jnp.tile(c64, (1, 2)), jnp.tile(s64, (1, 2))


def _route_and_ffn(h2p_list, aff_list, wg, wu, wd):
    n = h2p_list[0].shape[0]
    assert all(h.shape[0] == n for h in h2p_list)
    cap = EC_CAPACITY * n // N_EXPERTS
    gs, idxs = [], []
    for aff_t in aff_list:
        g, idx = lax.top_k(aff_t, cap)
        idx, g = lax.sort((idx.astype(jnp.int32), g), dimension=1, num_keys=1)
        gs.append(g)
        idxs.append(idx)
    idx_all = jnp.stack(idxs)
    ye = _expert_ffn(idx_all.reshape(-1), jnp.concatenate(h2p_list, axis=0), wg, wu, wd,
                     jnp.stack(gs)[..., None], n)
    return ye, idxs


def kernel(x_prompt, x_sample, cache_k, cache_v, c, c_ctx, w_mod, b_mod, norm1_g, norm2_g,
           attn_wqkv, attn_wo, lambda_q1, lambda_k1, lambda_q2, lambda_k2, subln_g,
           pool_w, pool_b, pool_scale, router_w, expert_w_gate, expert_w_up, expert_w_down,
           final_g):
    bp, n_ctx, d = x_prompt.shape
    bs, n_lat, _ = x_sample.shape
    past = cache_k.shape[2]
    n_p, n_s = bp * n_ctx, bs * n_lat

    rows = 8 * ((1 + bs + 7) // 8)
    cvecs = jnp.zeros((rows, d), F32).at[0].set(c_ctx).at[1:1 + bs].set(c)
    mod = _modulation_all(cvecs, w_mod, b_mod)
    rope_tabs = _rope_tables(n_lat)

    xp = x_prompt.reshape(n_p, d)
    xs = x_sample.reshape(n_s, d)
    new_k, new_v = [], []
    for l in range(DEPTH):
        mp_ = mod[l, 0:1].reshape(1, 6, 1, d)
        ms_ = mod[l, 1:1 + bs].reshape(bs, 6, 1, d)
        sp1, cp1, gp1, sp2, cp2, gp2 = [mp_[:, i] for i in range(6)]
        ss1, cs1, gs1, ss2, cs2, gs2 = [ms_[:, i] for i in range(6)]
        wr_t = router_w[l].T.astype(BF16)
        if l % 2 == 0:
            a = l // 2
            lam_init = 0.8 - 0.6 * math.exp(-0.3 * l)
            lamv = jnp.stack([lambda_q1[a], lambda_k1[a], lambda_q2[a], lambda_k2[a]])
            wqkv = attn_wqkv[a].astype(BF16)
            wo = attn_wo[a].astype(BF16)
            qp, kp, vp = _qkv_project(xp, norm1_g[l], sp1, cp1, wqkv, None, n_p, F32)
            new_k.append(kp.reshape(bp, n_ctx, N_HEADS, 2 * HEAD_DIM))
            new_v.append(vp.reshape(bp, n_ctx, N_HEADS, V_DIM))
            op = _diff_attention(qp, kp, vp, lamv, subln_g[a], lam_init, bp)
            xp, h2p, affp = _proj_mixer(op, wo, xp, gp1, norm2_g[l], sp2, cp2, wr_t, n_p)
            qs, ks_, vs = _qkv_project(xs, norm1_g[l], ss1, cs1, wqkv, rope_tabs, n_lat, BF16)
            k_all = jnp.concatenate([ks_.reshape(bs, n_lat, ATTN_WIDTH),
                                     cache_k[:, a].reshape(bs, past, ATTN_WIDTH).astype(BF16)],
                                    axis=1).reshape(bs * (n_lat + past), ATTN_WIDTH)
            v_all = jnp.concatenate([vs.reshape(bs, n_lat, ATTN_WIDTH),
                                     cache_v[:, a].reshape(bs, past, ATTN_WIDTH).astype(BF16)],
                                    axis=1).reshape(bs * (n_lat + past), ATTN_WIDTH)
            os_ = _diff_attention(qs, k_all, v_all, lamv, subln_g[a], lam_init, bs)
            xs, h2s, affs = _proj_mixer(os_, wo, xs, gs1, norm2_g[l], ss2, cs2, wr_t, n_lat)
        else:
            p = l // 2
            xp, h2p, affp = _pool_mixer(xp, norm1_g[l], sp1, cp1, pool_w[p], pool_b[p],
                                        pool_scale[p], gp1, norm2_g[l], sp2, cp2, wr_t,
                                        n_p, n_ctx)
            xs, h2s, affs = _pool_mixer(xs, norm1_g[l], ss1, cs1, pool_w[p], pool_b[p],
                                        pool_scale[p], gs1, norm2_g[l], ss2, cs2, wr_t,
                                        n_lat, n_lat)
        ye, idxs = _route_and_ffn([h2p, h2s], [affp, affs], expert_w_gate[l].astype(BF16),
                                  expert_w_up[l].astype(BF16), expert_w_down[l].astype(BF16))
        last = l == DEPTH - 1
        xp = _combine(ye[0], idxs[0][..., None], idxs[0], xp, gp2, final_g, n_p, last)
        xs = _combine(ye[1], idxs[1][..., None], idxs[1], xs, gs2, final_g, n_lat, last)
    return (xp.reshape(bp, n_ctx, d), xs.reshape(bs, n_lat, d),
            jnp.stack(new_k, axis=1), jnp.stack(new_v, axis=1))
```

```python
import math
import functools
import jax
import jax.numpy as jnp
from jax import lax
from jax.experimental import pallas as pl
from jax.experimental.pallas import tpu as pltpu

D_MODEL = 1024
DEPTH = 4
GRID_W = 64
N_HEADS = 8
HEAD_DIM = 64
V_DIM = 2 * HEAD_DIM
ATTN_WIDTH = N_HEADS * V_DIM
ROPE_BASE = 10000.0
N_POOL_GROUPS = 4
POOL_GROUP = D_MODEL // N_POOL_GROUPS
POOL_WINDOWS = (2, 4, 8, 16)
POOL_HALO = 8
N_EXPERTS = 16
EC_CAPACITY = 2
EPS = 1e-6

F32 = jnp.float32
BF16 = jnp.bfloat16
LANES = 128
LOG2E = 1.4426950408889634
VMEM_LIMIT = 56 * 1024 * 1024
ROW_TILE = 512
ATTN_TQ = 256
ATTN_TK = 1536
ATTN_HEADS_PER_STEP = 2
FFN_ROWS = 512
GATHER_UNROLL = 8
COMBINE_TM = 256
COMBINE_CH = 32
COMBINE_GROUP = 256


def _cparams(sem):
    return pltpu.CompilerParams(dimension_semantics=sem, vmem_limit_bytes=VMEM_LIMIT)


def _mod_kernel(c_ref, w_ref, b_ref, o_ref):
    c = c_ref[...]
    s = (c * (1.0 / (1.0 + jnp.exp(-c)))).astype(BF16)
    o_ref[...] = jnp.dot(s, w_ref[...].astype(BF16), preferred_element_type=F32) + b_ref[...]


def _modulation_all(cvecs, w_mod, b_mod):
    r, d = cvecs.shape
    depth, _, n6 = w_mod.shape
    tn = 1024
    return pl.pallas_call(
        _mod_kernel,
        grid=(depth, n6 // tn),
        in_specs=[pl.BlockSpec((r, d), lambda l, j: (0, 0)),
                  pl.BlockSpec((None, d, tn), lambda l, j: (l, 0, j)),
                  pl.BlockSpec((None, 1, tn), lambda l, j: (l, 0, j))],
        out_specs=pl.BlockSpec((None, r, tn), lambda l, j: (l, 0, j)),
        out_shape=jax.ShapeDtypeStruct((depth, r, n6), F32),
        compiler_params=_cparams(("parallel", "parallel")),
        name="modulation",
    )(cvecs, w_mod, b_mod.reshape(depth, 1, n6))


def _adaln_rows(x, g, shift, scale):
    ms = jnp.mean(x * x, axis=-1, keepdims=True)
    return (x * lax.rsqrt(ms + EPS)) * g * (1.0 + scale) + shift


def _swap16(x):
    lane = lax.broadcasted_iota(jnp.int32, x.shape, 1)
    up = pltpu.roll(x, x.shape[1] - 16, axis=1)
    dn = pltpu.roll(x, 16, axis=1)
    return jnp.where((lane % 32) < 16, up, dn)


def _qkv_kernel(*refs, rope):
    if rope:
        x_ref, g_ref, sh_ref, sc_ref, w_ref, cos_ref, sin_ref, q_ref, k_ref, v_ref = refs
    else:
        x_ref, g_ref, sh_ref, sc_ref, w_ref, q_ref, k_ref, v_ref = refs
    h = _adaln_rows(x_ref[...], g_ref[...], sh_ref[...], sc_ref[...]).astype(BF16)
    width = q_ref.shape[1]
    qscale = (HEAD_DIM ** -0.5) * LOG2E
    for part, o_ref in enumerate((q_ref, k_ref, v_ref)):
        r = jnp.dot(h, w_ref[:, part * width:(part + 1) * width], preferred_element_type=F32)
        if part < 2 and rope:
            c = cos_ref[...]
            s = sin_ref[...]
            for hd in range(width // LANES):
                blk = r[:, hd * LANES:(hd + 1) * LANES]
                blk = blk * c + _swap16(blk) * s
                if part == 0:
                    blk = blk * qscale
                o_ref[:, hd * LANES:(hd + 1) * LANES] = blk.astype(o_ref.dtype)
        else:
            if part == 0:
                r = r * qscale
            o_ref[...] = r.astype(o_ref.dtype)


def _qkv_project(x2d, g, shift, scale, w_bf16, rope_tabs, tokens_per_mod, kv_dtype):
    n, d = x2d.shape
    tm = ROW_TILE
    width = w_bf16.shape[1] // 3
    rope = rope_tabs is not None
    blocks_per_mod = tokens_per_mod // tm
    in_specs = [pl.BlockSpec((tm, d), lambda i: (i, 0)),
                pl.BlockSpec((1, d), lambda i: (0, 0)),
                pl.BlockSpec((None, 1, d), lambda i: (i // blocks_per_mod, 0, 0)),
                pl.BlockSpec((None, 1, d), lambda i: (i // blocks_per_mod, 0, 0)),
                pl.BlockSpec((d, 3 * width), lambda i: (0, 0))]
    args = [x2d, g.reshape(1, d), shift, scale, w_bf16]
    if rope:
        cos_t, sin_t = rope_tabs
        seq_blocks = cos_t.shape[0] // tm
        in_specs += [pl.BlockSpec((tm, LANES), lambda i: (i % seq_blocks, 0)),
                     pl.BlockSpec((tm, LANES), lambda i: (i % seq_blocks, 0))]
        args += [cos_t, sin_t]
    out_spec = pl.BlockSpec((tm, width), lambda i: (i, 0))
    return pl.pallas_call(
        functools.partial(_qkv_kernel, rope=rope),
        grid=(n // tm,),
        in_specs=in_specs,
        out_specs=[out_spec, out_spec, out_spec],
        out_shape=[jax.ShapeDtypeStruct((n, width), BF16),
                   jax.ShapeDtypeStruct((n, width), kv_dtype),
                   jax.ShapeDtypeStruct((n, width), kv_dtype)],
        compiler_params=_cparams(("parallel",)),
        name="adaln_qkv_rope" if rope else "adaln_qkv",
    )(*args)


def _attn_kernel(lamv_ref, q_ref, k_ref, v_ref, g_ref, o_ref, qs_ref, m_ref, l_ref, acc_ref,
                 *, tk, lam_init, hp):
    tq = q_ref.shape[0]
    nk = k_ref.shape[0]
    for h in range(hp):
        q = q_ref[:, h * LANES:(h + 1) * LANES]
        lane = lax.broadcasted_iota(jnp.int32, q.shape, 1)
        zero = jnp.zeros_like(q)
        qs_ref[h, 0:tq, :] = jnp.where(lane < HEAD_DIM, q, zero)
        qs_ref[h, tq:2 * tq, :] = jnp.where(lane >= HEAD_DIM, q, zero)
    m_ref[...] = jnp.full(m_ref.shape, -jnp.inf, F32)
    l_ref[...] = jnp.zeros(l_ref.shape, F32)
    acc_ref[...] = jnp.zeros(acc_ref.shape, F32)

    def body(j, carry):
        off = pl.multiple_of(j * tk, tk)
        for h in range(hp):
            kc = k_ref[pl.ds(off, tk), h * LANES:(h + 1) * LANES].astype(BF16)
            vc = v_ref[pl.ds(off, tk), h * LANES:(h + 1) * LANES].astype(BF16)
            s = lax.dot_general(qs_ref[h], kc, (((1,), (1,)), ((), ())),
                                preferred_element_type=F32)
            m_old = m_ref[h]
            m_new = jnp.maximum(m_old, jnp.max(s, axis=1)[:, None])
            alpha = jnp.exp2(m_old - m_new)
            p = jnp.exp2(s - jnp.tile(m_new, (1, tk // LANES)))
            l_ref[h] = alpha * l_ref[h] + jnp.sum(p, axis=1)[:, None]
            acc_ref[h] = alpha * acc_ref[h] + jnp.dot(p.astype(BF16), vc,
                                                      preferred_element_type=F32)
            m_ref[h] = m_new
        return carry

    lax.fori_loop(0, nk // tk, body, 0)

    lv = lamv_ref[...]
    lam = (jnp.exp(jnp.sum(lv[0:1, :] * lv[1:2, :], axis=-1, keepdims=True))
           - jnp.exp(jnp.sum(lv[2:3, :] * lv[3:4, :], axis=-1, keepdims=True)) + lam_init)
    for h in range(hp):
        o = acc_ref[h] / l_ref[h]
        d = o[0:tq, :] - lam * o[tq:2 * tq, :]
        ms = jnp.mean(d * d, axis=-1, keepdims=True)
        o_ref[:, h * LANES:(h + 1) * LANES] = (
            (d * lax.rsqrt(ms + EPS)) * g_ref[...] * (1.0 - lam_init)).astype(o_ref.dtype)


def _diff_attention(q, k, v, lamv, subln_g, lam_init, batch):
    nq = q.shape[0] // batch
    nk = k.shape[0] // batch
    tq = min(ATTN_TQ, nq)
    tk = ATTN_TK if nk % ATTN_TK == 0 else nk
    assert nq % tq == 0 and nk % tk == 0 and tk % LANES == 0
    hp = ATTN_HEADS_PER_STEP
    w = hp * LANES
    q3 = q.reshape(batch, nq, ATTN_WIDTH)
    k3 = k.reshape(batch, nk, ATTN_WIDTH)
    v3 = v.reshape(batch, nk, ATTN_WIDTH)
    out = pl.pallas_call(
        functools.partial(_attn_kernel, tk=tk, lam_init=lam_init, hp=hp),
        grid=(batch, N_HEADS // hp, nq // tq),
        in_specs=[pl.BlockSpec((4, HEAD_DIM), lambda b, h, i: (0, 0)),
                  pl.BlockSpec((None, tq, w), lambda b, h, i: (b, i, h)),
                  pl.BlockSpec((None, nk, w), lambda b, h, i: (b, 0, h)),
                  pl.BlockSpec((None, nk, w), lambda b, h, i: (b, 0, h)),
                  pl.BlockSpec((1, V_DIM), lambda b, h, i: (0, 0))],
        out_specs=pl.BlockSpec((None, tq, w), lambda b, h, i: (b, i, h)),
        out_shape=jax.ShapeDtypeStruct((batch, nq, ATTN_WIDTH), BF16),
        scratch_shapes=[pltpu.VMEM((hp, 2 * tq, V_DIM), BF16),
                        pltpu.VMEM((hp, 2 * tq, LANES), F32),
                        pltpu.VMEM((hp, 2 * tq, LANES), F32),
                        pltpu.VMEM((hp, 2 * tq, V_DIM), F32)],
        compiler_params=_cparams(("parallel", "parallel", "parallel")),
        name="diff_attention",
    )(lamv, q3, k3, v3, subln_g.reshape(1, V_DIM))
    return out.reshape(batch * nq, ATTN_WIDTH)


def _epilogue(x_new, g2, sh2, sc2, wr_t, xo_ref, h2_ref, aff_ref):
    xo_ref[...] = x_new
    h2 = _adaln_rows(x_new, g2, sh2, sc2)
    h2b = h2.astype(BF16)
    for j in range(h2_ref.shape[1]):
        h2_ref[:, j, :] = h2[:, j * LANES:(j + 1) * LANES]
    logits = lax.dot_general(wr_t, h2b, (((1,), (1,)), ((), ())),
                             preferred_element_type=F32)
    mx = jnp.max(logits, axis=0, keepdims=True)
    ex = jnp.exp(logits - mx)
    aff = ex / jnp.sum(ex, axis=0, keepdims=True)
    for r in range(aff_ref.shape[0]):
        aff_ref[r] = aff[:, r * LANES:(r + 1) * LANES]


def _proj_kernel(o_ref, w_ref, x_ref, gate_ref, g2_ref, sh2_ref, sc2_ref, wr_ref,
                 xo_ref, h2_ref, aff_ref):
    y = jnp.dot(o_ref[...], w_ref[...], preferred_element_type=F32)
    x_new = x_ref[...] + gate_ref[...] * y
    _epilogue(x_new, g2_ref[...], sh2_ref[...], sc2_ref[...], wr_ref[...], xo_ref, h2_ref, aff_ref)


def _epilogue_specs(d, tm, blocks_per_mod):
    mod_spec = pl.BlockSpec((None, 1, d), lambda i: (i // blocks_per_mod, 0, 0))
    in_specs = [pl.BlockSpec((1, d), lambda i: (0, 0)), mod_spec, mod_spec,
                pl.BlockSpec((N_EXPERTS, d), lambda i: (0, 0))]
    out_specs = [pl.BlockSpec((tm, d), lambda i: (i, 0)),
                 pl.BlockSpec((tm, d // LANES, LANES), lambda i: (i, 0, 0)),
                 pl.BlockSpec((tm // LANES, N_EXPERTS, LANES), lambda i: (i, 0, 0))]
    return in_specs, out_specs


def _epilogue_out_shape(n, d):
    return [jax.ShapeDtypeStruct((n, d), F32),
            jax.ShapeDtypeStruct((n, d // LANES, LANES), F32),
            jax.ShapeDtypeStruct((n // LANES, N_EXPERTS, LANES), F32)]


def _proj_mixer(o, w_bf16, x2d, gate1, g2, sh2, sc2, wr_t, tokens_per_mod):
    n, d = x2d.shape
    tm = ROW_TILE
    kdim = o.shape[1]
    bpm = tokens_per_mod // tm
    ep_in, ep_out = _epilogue_specs(d, tm, bpm)
    return pl.pallas_call(
        _proj_kernel,
        grid=(n // tm,),
        in_specs=[pl.BlockSpec((tm, kdim), lambda i: (i, 0)),
                  pl.BlockSpec((kdim, d), lambda i: (0, 0)),
                  pl.BlockSpec((tm, d), lambda i: (i, 0)),
                  pl.BlockSpec((None, 1, d), lambda i: (i // bpm, 0, 0))] + ep_in,
        out_specs=ep_out,
        out_shape=_epilogue_out_shape(n, d),
        compiler_params=_cparams(("parallel",)),
        name="proj_mixer",
    )(o, w_bf16, x2d, gate1, g2.reshape(1, d), sh2, sc2, wr_t)


def _pool_kernel(x_ref, xprev_ref, xnext_ref, g1_ref, sh1_ref, sc1_ref, pw_ref, pb_ref, ps_ref,
                 gate_ref, g2_ref, sh2_ref, sc2_ref, wr_ref, xo_ref, h2_ref, aff_ref, *, seq_len):
    tm, d = x_ref.shape
    halo = POOL_HALO
    i = pl.program_id(0)
    pos0 = (i * tm) % seq_len
    g1, sh1, sc1 = g1_ref[...], sh1_ref[...], sc1_ref[...]
    x = x_ref[...]
    h = _adaln_rows(x, g1, sh1, sc1)
    hprev = _adaln_rows(xprev_ref[...], g1, sh1, sc1)
    hnext = _adaln_rows(xnext_ref[...], g1, sh1, sc1)
    hprev = jnp.where(pos0 > 0, hprev, 0.0)
    hnext = jnp.where(pos0 + tm < seq_len, hnext, 0.0)
    hext = jnp.concatenate([hprev, h, hnext], axis=0)
    rows = tm + 2 * halo
    t = pos0 + lax.broadcasted_iota(jnp.int32, (tm, 1), 0)
    outs = []
    for gi, win in enumerate(POOL_WINDOWS):
        a = hext[:, gi * POOL_GROUP:(gi + 1) * POOL_GROUP]
        p = pltpu.roll(a, 1, axis=0) + a
        step = 1
        while 2 * step < win:
            p = pltpu.roll(p, step, axis=0) + pltpu.roll(p, rows - step, axis=0)
            step *= 2
        half = win // 2
        cnt = (jnp.minimum(t + half, seq_len) - jnp.maximum(t - half, 0)).astype(F32)
        pooled = p[halo:halo + tm, :] / cnt - a[halo:halo + tm, :]
        y = jnp.dot(pooled.astype(BF16), pw_ref[gi].astype(BF16), preferred_element_type=F32)
        outs.append(y + pb_ref[gi:gi + 1, :])
    mix = jnp.concatenate(outs, axis=1) * ps_ref[...]
    x_new = x + gate_ref[...] * mix
    _epilogue(x_new, g2_ref[...], sh2_ref[...], sc2_ref[...], wr_ref[...], xo_ref, h2_ref, aff_ref)


def _pool_mixer(x2d, g1, sh1, sc1, pool_w, pool_b, pool_scale, gate1, g2, sh2, sc2, wr_t,
                tokens_per_mod, seq_len):
    n, d = x2d.shape
    tm = min(ROW_TILE, seq_len)
    bpm = tokens_per_mod // tm
    hb = tm // POOL_HALO
    n_hblocks = n // POOL_HALO
    mod_spec = pl.BlockSpec((None, 1, d), lambda i: (i // bpm, 0, 0))
    ep_in, ep_out = _epilogue_specs(d, tm, bpm)
    return pl.pallas_call(
        functools.partial(_pool_kernel, seq_len=seq_len),
        grid=(n // tm,),
        in_specs=[pl.BlockSpec((tm, d), lambda i: (i, 0)),
                  pl.BlockSpec((POOL_HALO, d), lambda i: (jnp.maximum(i * hb - 1, 0), 0)),
                  pl.BlockSpec((POOL_HALO, d),
                               lambda i: (jnp.minimum((i + 1) * hb, n_hblocks - 1), 0)),
                  pl.BlockSpec((1, d), lambda i: (0, 0)), mod_spec, mod_spec,
                  pl.BlockSpec((N_POOL_GROUPS, POOL_GROUP, POOL_GROUP), lambda i: (0, 0, 0)),
                  pl.BlockSpec((N_POOL_GROUPS, POOL_GROUP), lambda i: (0, 0)),
                  pl.BlockSpec((1, d), lambda i: (0, 0)),
                  mod_spec] + ep_in,
        out_specs=ep_out,
        out_shape=_epilogue_out_shape(n, d),
        compiler_params=_cparams(("parallel",)),
        name="pool_mixer",
    )(x2d, x2d, x2d, g1.reshape(1, d), sh1, sc1, pool_w, pool_b, pool_scale.reshape(1, d),
      gate1, g2.reshape(1, d), sh2, sc2, wr_t)


def _ffn_gather_copy(h2p_hbm, xbuf, sem, slot, src_row, dst_row, nrows):
    return pltpu.make_async_copy(h2p_hbm.at[pl.ds(src_row, nrows)],
                                 xbuf.at[slot, pl.ds(dst_row, nrows)], sem.at[slot])


def _ffn_kernel(idx_ref, h2p_hbm, wg_ref, wu_ref, wd_ref, g_ref, ye_ref, xbuf, sem,
                *, rows, n_tokens):
    n_e, n_s = pl.num_programs(0), pl.num_programs(1)
    ei, si = pl.program_id(0), pl.program_id(1)
    cap, pieces = xbuf.shape[1], xbuf.shape[2]
    step = ei * n_s + si
    slot = step % 2

    def issue(e_t, s_t, slot_t):
        base = (s_t * n_e + e_t) * cap
        row0 = s_t * n_tokens

        def body(kb, carry):
            k0 = kb * GATHER_UNROLL
            for u in range(GATHER_UNROLL):
                _ffn_gather_copy(h2p_hbm, xbuf, sem, slot_t, row0 + idx_ref[base + k0 + u],
                                 k0 + u, 1).start()
            return carry

        lax.fori_loop(0, cap // GATHER_UNROLL, body, 0)

    @pl.when(step == 0)
    def _():
        issue(ei, si, slot)

    @pl.when(step + 1 < n_e * n_s)
    def _():
        nxt = step + 1
        issue(nxt // n_s, nxt % n_s, 1 - slot)

    _ffn_gather_copy(h2p_hbm, xbuf, sem, slot, 0, 0, cap).wait()

    def body(i, carry):
        r0 = pl.multiple_of(i * rows, rows)
        x = jnp.concatenate([xbuf[slot, pl.ds(r0, rows), j, :].astype(BF16)
                             for j in range(pieces)], axis=1)
        a = jnp.dot(x, wg_ref[...], preferred_element_type=F32)
        b = jnp.dot(x, wu_ref[...], preferred_element_type=F32)
        hid = (a * (1.0 / (1.0 + jnp.exp(-a))) * b).astype(BF16)
        y = jnp.dot(hid, wd_ref[...], preferred_element_type=F32)
        ye_ref[pl.ds(r0, rows), :] = (y * g_ref[pl.ds(r0, rows), :]).astype(ye_ref.dtype)
        return carry

    lax.fori_loop(0, cap // rows, body, 0)


def _expert_ffn(idx_flat, h2p_all, wg, wu, wd, gates, n_tokens, layer):
    s, e, c, _ = gates.shape
    d, f = wg.shape[2], wg.shape[3]
    return pl.pallas_call(
        functools.partial(_ffn_kernel, rows=min(FFN_ROWS, c), n_tokens=n_tokens),
        grid_spec=pltpu.PrefetchScalarGridSpec(
            num_scalar_prefetch=1,
            grid=(e, s),
            in_specs=[pl.BlockSpec(memory_space=pl.ANY),
                      pl.BlockSpec((None, None, d, f), lambda ei, si, idx: (layer, ei, 0, 0)),
                      pl.BlockSpec((None, None, d, f), lambda ei, si, idx: (layer, ei, 0, 0)),
                      pl.BlockSpec((None, None, f, d), lambda ei, si, idx: (layer, ei, 0, 0)),
                      pl.BlockSpec((None, None, c, 1), lambda ei, si, idx: (si, ei, 0, 0))],
            out_specs=pl.BlockSpec((None, None, c, d), lambda ei, si, idx: (si, ei, 0, 0)),
            scratch_shapes=[pltpu.VMEM((2, c, d // LANES, LANES), F32),
                            pltpu.SemaphoreType.DMA((2,))]),
        out_shape=jax.ShapeDtypeStruct((s, e, c, d), BF16),
        compiler_params=pltpu.CompilerParams(dimension_semantics=("arbitrary", "arbitrary"),
                                             vmem_limit_bytes=VMEM_LIMIT,
                                             disable_bounds_checks=True),
        name="expert_ffn",
    )(idx_flat, h2p_all, wg, wu, wd, gates)


def _combine_kernel(clo_ref, ncnt_ref, ye_hbm, tok_hbm, x_ref, gate_ref, fg_ref, o_ref,
                    zbuf, tokbuf, acc_ref, sem, *, final):
    n_e = ye_hbm.shape[0]
    tm, d = x_ref.shape
    ch, grp = COMBINE_CH, COMBINE_GROUP
    i = pl.program_id(0)
    nt = pl.num_programs(0)
    slot = i % 2

    def chunk_copies(e, src_row, slot_t, dst_row):
        return (pltpu.make_async_copy(ye_hbm.at[e, pl.ds(src_row, ch), :],
                                      zbuf.at[slot_t, pl.ds(dst_row, ch), :], sem.at[slot_t]),
                pltpu.make_async_copy(tok_hbm.at[e, pl.ds(src_row, ch), :],
                                      tokbuf.at[slot_t, pl.ds(dst_row, ch), :], sem.at[slot_t]))

    def issue(tile, slot_t):
        pos = 0
        for e in range(n_e):
            lo = clo_ref[tile * n_e + e]

            def body(j, p, e=e, lo=lo):
                src = pl.multiple_of((lo + j) * ch, ch)
                dst = pl.multiple_of(p * ch, ch)
                for cp in chunk_copies(e, src, slot_t, dst):
                    cp.start()
                return p + 1

            pos = lax.fori_loop(0, ncnt_ref[tile * n_e + e], body, pos)

    def total_chunks(tile):
        tot = 0
        for e in range(n_e):
            tot = tot + ncnt_ref[tile * n_e + e]
        return tot

    @pl.when(i == 0)
    def _():
        zbuf[...] = jnp.zeros(zbuf.shape, zbuf.dtype)
        tokbuf[...] = jnp.full(tokbuf.shape, -1, jnp.int32)
        issue(i, slot)

    @pl.when(i + 1 < nt)
    def _():
        issue(i + 1, 1 - slot)

    total = total_chunks(i)

    def wait_body(j, carry):
        for cp in chunk_copies(0, 0, slot, 0):
            cp.wait()
        return carry

    lax.fori_loop(0, total, wait_body, 0)

    acc_ref[...] = jnp.zeros(acc_ref.shape, F32)
    tok_of_lane = i * tm + lax.broadcasted_iota(jnp.int32, (grp, tm), 1)
    row_in_group = lax.broadcasted_iota(jnp.int32, (grp, 1), 0)

    def group_body(gidx, carry):
        r0 = pl.multiple_of(gidx * grp, grp)
        tok = jnp.where(r0 + row_in_group < total * ch, tokbuf[slot, pl.ds(r0, grp), :], -1)
        onehot_t = (tok == tok_of_lane).astype(BF16)
        acc_ref[...] += lax.dot_general(onehot_t, zbuf[slot, pl.ds(r0, grp), :],
                                        (((0,), (0,)), ((), ())), preferred_element_type=F32)
        return carry

    lax.fori_loop(0, (total * ch + grp - 1) // grp, group_body, 0)

    x = x_ref[...] + gate_ref[...] * acc_ref[...]
    if final:
        ms = jnp.mean(x * x, axis=-1, keepdims=True)
        x = x * lax.rsqrt(ms + EPS) * fg_ref[...]
    o_ref[...] = x


def _combine(ye, tok_col, idx_sorted, x2d, gate2, final_g, tokens_per_mod, final):
    n, d = x2d.shape
    e, c, _ = ye.shape
    tm, ch = COMBINE_TM, COMBINE_CH
    n_tiles = n // tm
    bpm = tokens_per_mod // tm
    bounds = jnp.arange(n_tiles + 1, dtype=jnp.int32) * tm
    first = jnp.sum(idx_sorted[:, :, None] < bounds[None, None, :], axis=1).astype(jnp.int32)
    clo = first[:, :-1] // ch
    chi = (first[:, 1:] + ch - 1) // ch
    ncnt = jnp.maximum(chi - clo, 0)
    max_rows = e * (tm // ch + 2) * ch
    max_rows = ((max_rows + COMBINE_GROUP - 1) // COMBINE_GROUP) * COMBINE_GROUP
    return pl.pallas_call(
        functools.partial(_combine_kernel, final=final),
        grid_spec=pltpu.PrefetchScalarGridSpec(
            num_scalar_prefetch=2,
            grid=(n_tiles,),
            in_specs=[pl.BlockSpec(memory_space=pl.ANY),
                      pl.BlockSpec(memory_space=pl.ANY),
                      pl.BlockSpec((tm, d), lambda i, a, b: (i, 0)),
                      pl.BlockSpec((None, 1, d), lambda i, a, b: (i // bpm, 0, 0)),
                      pl.BlockSpec((1, d), lambda i, a, b: (0, 0))],
            out_specs=pl.BlockSpec((tm, d), lambda i, a, b: (i, 0)),
            scratch_shapes=[pltpu.VMEM((2, max_rows, d), BF16),
                            pltpu.VMEM((2, max_rows, 1), jnp.int32),
                            pltpu.VMEM((tm, d), F32),
                            pltpu.SemaphoreType.DMA((2,))]),
        out_shape=jax.ShapeDtypeStruct((n, d), F32),
        compiler_params=_cparams(("arbitrary",)),
        name="combine_final" if final else "combine",
    )(clo.T.reshape(-1), ncnt.T.reshape(-1), ye, tok_col, x2d, gate2, final_g.reshape(1, d))


def _rope_tables(n_tokens):
    t = jnp.arange(n_tokens)
    row = (t // GRID_W).astype(F32)
    col = (t % GRID_W).astype(F32)
    half = HEAD_DIM // 2
    inv = ROPE_BASE ** (-jnp.arange(0, half, 2, dtype=F32) / half)
    ar, ac = row[:, None] * inv, col[:, None] * inv
    c64 = jnp.concatenate([jnp.cos(ar), jnp.cos(ar), jnp.cos(ac), jnp.cos(ac)], axis=-1)
    s64 = jnp.concatenate([-jnp.sin(ar), jnp.sin(ar), -jnp.sin(ac), jnp.sin(ac)], axis=-1)
    return jnp.tile(c64, (1, 2)), jnp.tile(s64, (1, 2))


def _tile_prefix(mask3, il_ref, to_ref, tot_ref):
    nt, ne, ln = mask3.shape
    m2 = mask3.astype(F32).astype(BF16).reshape(nt * ne, ln)
    r = lax.broadcasted_iota(jnp.int32, (ln, ln), 0)
    c = lax.broadcasted_iota(jnp.int32, (ln, ln), 1)
    upper = (r <= c).astype(F32).astype(BF16)
    ones = jnp.ones((ln, ln), BF16)
    il_ref[...] = jnp.dot(m2, upper, preferred_element_type=F32).reshape(nt, ne, ln)
    tot_ref[...] = jnp.dot(m2, ones, preferred_element_type=F32).reshape(nt, ne, ln)
    run = jnp.zeros((ne, ln), F32)
    for tt in range(nt):
        to_ref[tt] = run
        run = run + tot_ref[tt]


def _route_kernel(aff_ref, idx_ref, g_ref, il_ref, to_ref, tot_ref, *, cap):
    nt, ne, ln = aff_ref.shape
    a3 = aff_ref[...]

    def count(mask3):
        per_lane = jnp.sum(mask3.astype(jnp.int32), axis=0)
        return jnp.sum(per_lane, axis=1, keepdims=True)

    def as_float(thr_bits):
        return pltpu.bitcast(jnp.broadcast_to(thr_bits, (ne, ln)), F32)[:, 0:1]

    def search(i, thr_bits):
        cand = thr_bits | lax.shift_left(jnp.int32(1), 30 - i)
        return jnp.where(count(a3 >= as_float(cand)[None]) >= cap, cand, thr_bits)

    thr = as_float(lax.fori_loop(0, 31, search, jnp.zeros((ne, 1), jnp.int32)))
    gt = a3 > thr[None]
    eq = a3 == thr[None]
    need = (cap - count(gt)).astype(F32)
    _tile_prefix(eq, il_ref, to_ref, tot_ref)
    eq_before = to_ref[...] + il_ref[...] - eq.astype(F32)
    sel = gt | (eq & (eq_before < need[None]))
    _tile_prefix(sel, il_ref, to_ref, tot_ref)

    reps = cap // ln
    slot = lax.broadcasted_iota(jnp.int32, (nt, cap), 1).astype(F32)
    tile_id = lax.broadcasted_iota(jnp.int32, (nt, cap), 0).astype(F32)
    in_tile = lax.broadcasted_iota(jnp.int32, (ln, cap), 0).astype(F32)
    for e in range(ne):
        il_e = il_ref[:, e, :]
        to_e = jnp.tile(to_ref[:, e, :], (1, reps))
        tot_e = jnp.tile(tot_ref[:, e, :], (1, reps))
        owns = (to_e <= slot) & (slot < to_e + tot_e)
        rank = jnp.sum(jnp.where(owns, slot - to_e, 0.0), axis=0, keepdims=True)
        tile_of = jnp.sum(jnp.where(owns, tile_id, 0.0), axis=0, keepdims=True)
        owns_b = owns.astype(F32).astype(BF16)
        contract0 = (((0,), (0,)), ((), ()))
        il_g = lax.dot_general(il_e.astype(BF16), owns_b, contract0,
                               preferred_element_type=F32)
        pos = jnp.sum((il_g <= rank).astype(F32), axis=0, keepdims=True)
        idx_ref[e:e + 1, :] = (tile_of * ln + pos).astype(jnp.int32)
        a_e = aff_ref[:, e, :]
        p1 = a_e.astype(BF16)
        r1 = a_e - p1.astype(F32)
        p2 = r1.astype(BF16)
        p3 = (r1 - p2.astype(F32)).astype(BF16)
        a_g = (lax.dot_general(p1, owns_b, contract0, preferred_element_type=F32)
               + lax.dot_general(p2, owns_b, contract0, preferred_element_type=F32)
               + lax.dot_general(p3, owns_b, contract0, preferred_element_type=F32))
        g_ref[e:e + 1, :] = jnp.sum(jnp.where(in_tile == pos, a_g, 0.0), axis=0, keepdims=True)


def _route(aff_all, cap):
    s, nt, ne, ln = aff_all.shape
    return pl.pallas_call(
        functools.partial(_route_kernel, cap=cap),
        grid=(s,),
        in_specs=[pl.BlockSpec((None, nt, ne, ln), lambda i: (i, 0, 0, 0))],
        out_specs=[pl.BlockSpec((None, ne, cap), lambda i: (i, 0, 0)),
                   pl.BlockSpec((None, ne, cap), lambda i: (i, 0, 0))],
        out_shape=[jax.ShapeDtypeStruct((s, ne, cap), jnp.int32),
                   jax.ShapeDtypeStruct((s, ne, cap), F32)],
        scratch_shapes=[pltpu.VMEM((nt, ne, ln), F32)] * 3,
        compiler_params=_cparams(("parallel",)),
        name="route",
    )(aff_all)


def _route_and_ffn(h2_list, aff_list, w_gate, w_up, w_down, layer):
    n = h2_list[0].shape[0]
    assert all(h.shape[0] == n for h in h2_list)
    cap = EC_CAPACITY * n // N_EXPERTS
    idx_all, g_all = _route(jnp.stack(aff_list), cap)
    ye = _expert_ffn(idx_all.reshape(-1), jnp.concatenate(h2_list, axis=0), w_gate, w_up, w_down,
                     g_all[..., None], n, layer)
    return ye, idx_all


def kernel(x_prompt, x_sample, cache_k, cache_v, c, c_ctx, w_mod, b_mod, norm1_g, norm2_g,
           attn_wqkv, attn_wo, lambda_q1, lambda_k1, lambda_q2, lambda_k2, subln_g,
           pool_w, pool_b, pool_scale, router_w, expert_w_gate, expert_w_up, expert_w_down,
           final_g):
    bp, n_ctx, d = x_prompt.shape
    bs, n_lat, _ = x_sample.shape
    past = cache_k.shape[2]
    n_p, n_s = bp * n_ctx, bs * n_lat

    rows = 8 * ((1 + bs + 7) // 8)
    cvecs = jnp.zeros((rows, d), F32).at[0].set(c_ctx).at[1:1 + bs].set(c)
    mod = _modulation_all(cvecs, w_mod, b_mod)
    rope_tabs = _rope_tables(n_lat)
    w_gate_b = expert_w_gate.astype(BF16)
    w_up_b = expert_w_up.astype(BF16)
    w_down_b = expert_w_down.astype(BF16)

    xp = x_prompt.reshape(n_p, d)
    xs = x_sample.reshape(n_s, d)
    new_k, new_v = [], []
    for l in range(DEPTH):
        mp_ = mod[l, 0:1].reshape(1, 6, 1, d)
        ms_ = mod[l, 1:1 + bs].reshape(bs, 6, 1, d)
        sp1, cp1, gp1, sp2, cp2, gp2 = [mp_[:, i] for i in range(6)]
        ss1, cs1, gs1, ss2, cs2, gs2 = [ms_[:, i] for i in range(6)]
        wr_t = router_w[l].T.astype(BF16)
        if l % 2 == 0:
            a = l // 2
            lam_init = 0.8 - 0.6 * math.exp(-0.3 * l)
            lamv = jnp.stack([lambda_q1[a], lambda_k1[a], lambda_q2[a], lambda_k2[a]])
            wqkv = attn_wqkv[a].astype(BF16)
            wo = attn_wo[a].astype(BF16)
            qp, kp, vp = _qkv_project(xp, norm1_g[l], sp1, cp1, wqkv, None, n_p, F32)
            new_k.append(kp.reshape(bp, n_ctx, N_HEADS, 2 * HEAD_DIM))
            new_v.append(vp.reshape(bp, n_ctx, N_HEADS, V_DIM))
            op = _diff_attention(qp, kp, vp, lamv, subln_g[a], lam_init, bp)
            xp, h2p, affp = _proj_mixer(op, wo, xp, gp1, norm2_g[l], sp2, cp2, wr_t, n_p)
            qs, ks_, vs = _qkv_project(xs, norm1_g[l], ss1, cs1, wqkv, rope_tabs, n_lat, BF16)
            k_all = jnp.concatenate([ks_.reshape(bs, n_lat, ATTN_WIDTH),
                                     cache_k[:, a].reshape(bs, past, ATTN_WIDTH).astype(BF16)],
                                    axis=1).reshape(bs * (n_lat + past), ATTN_WIDTH)
            v_all = jnp.concatenate([vs.reshape(bs, n_lat, ATTN_WIDTH),
                                     cache_v[:, a].reshape(bs, past, ATTN_WIDTH).astype(BF16)],
                                    axis=1).reshape(bs * (n_lat + past), ATTN_WIDTH)
            os_ = _diff_attention(qs, k_all, v_all, lamv, subln_g[a], lam_init, bs)
            xs, h2s, affs = _proj_mixer(os_, wo, xs, gs1, norm2_g[l], ss2, cs2, wr_t, n_lat)
        else:
            p = l // 2
            xp, h2p, affp = _pool_mixer(xp, norm1_g[l], sp1, cp1, pool_w[p], pool_b[p],
                                        pool_scale[p], gp1, norm2_g[l], sp2, cp2, wr_t,
                                        n_p, n_ctx)
            xs, h2s, affs = _pool_mixer(xs, norm1_g[l], ss1, cs1, pool_w[p], pool_b[p],
                                        pool_scale[p], gs1, norm2_g[l], ss2, cs2, wr_t,
                                        n_lat, n_lat)
        ye, idxs = _route_and_ffn([h2p, h2s], [affp, affs], w_gate_b, w_up_b, w_down_b, l)
        last = l == DEPTH - 1
        xp = _combine(ye[0], idxs[0][..., None], idxs[0], xp, gp2, final_g, n_p, last)
        xs = _combine(ye[1], idxs[1][..., None], idxs[1], xs, gs2, final_g, n_lat, last)
    return (xp.reshape(bp, n_ctx, d), xs.reshape(bs, n_lat, d),
            jnp.stack(new_k, axis=1), jnp.stack(new_v, axis=1))
```

```python
import math
import functools
import jax
import jax.numpy as jnp
from jax import lax
from jax.experimental import pallas as pl
from jax.experimental.pallas import tpu as pltpu

D_MODEL = 1024
DEPTH = 4
GRID_W = 64
N_HEADS = 8
HEAD_DIM = 64
V_DIM = 2 * HEAD_DIM
ATTN_WIDTH = N_HEADS * V_DIM
ROPE_BASE = 10000.0
N_POOL_GROUPS = 4
POOL_GROUP = D_MODEL // N_POOL_GROUPS
POOL_WINDOWS = (2, 4, 8, 16)
POOL_HALO = 8
N_EXPERTS = 16
EC_CAPACITY = 2
EPS = 1e-6

F32 = jnp.float32
BF16 = jnp.bfloat16
LANES = 128
LOG2E = 1.4426950408889634
VMEM_LIMIT = 56 * 1024 * 1024
ROW_TILE = 512
ATTN_TQ = 256
ATTN_TK = 1536
ATTN_HEADS_PER_STEP = 2
FFN_ROWS = 512
GATHER_UNROLL = 8
COMBINE_TM = 256
COMBINE_CH = 32
COMBINE_GROUP = 256


def _cparams(sem):
    return pltpu.CompilerParams(dimension_semantics=sem, vmem_limit_bytes=VMEM_LIMIT)


def _mod_kernel(c_ref, w_ref, b_ref, o_ref):
    c = c_ref[...]
    s = (c * (1.0 / (1.0 + jnp.exp(-c)))).astype(BF16)
    o_ref[...] = jnp.dot(s, w_ref[...].astype(BF16), preferred_element_type=F32) + b_ref[...]


def _modulation_all(cvecs, w_mod, b_mod):
    r, d = cvecs.shape
    depth, _, n6 = w_mod.shape
    tn = 1024
    return pl.pallas_call(
        _mod_kernel,
        grid=(depth, n6 // tn),
        in_specs=[pl.BlockSpec((r, d), lambda l, j: (0, 0)),
                  pl.BlockSpec((None, d, tn), lambda l, j: (l, 0, j)),
                  pl.BlockSpec((None, 1, tn), lambda l, j: (l, 0, j))],
        out_specs=pl.BlockSpec((None, r, tn), lambda l, j: (l, 0, j)),
        out_shape=jax.ShapeDtypeStruct((depth, r, n6), F32),
        compiler_params=_cparams(("parallel", "parallel")),
        name="modulation",
    )(cvecs, w_mod, b_mod.reshape(depth, 1, n6))


def _adaln_rows(x, g, shift, scale):
    ms = jnp.mean(x * x, axis=-1, keepdims=True)
    return (x * lax.rsqrt(ms + EPS)) * g * (1.0 + scale) + shift


def _swap16(x):
    lane = lax.broadcasted_iota(jnp.int32, x.shape, 1)
    up = pltpu.roll(x, x.shape[1] - 16, axis=1)
    dn = pltpu.roll(x, 16, axis=1)
    return jnp.where((lane % 32) < 16, up, dn)


def _qkv_kernel(*refs, rope):
    if rope:
        x_ref, g_ref, sh_ref, sc_ref, w_ref, cos_ref, sin_ref, q_ref, k_ref, v_ref = refs
    else:
        x_ref, g_ref, sh_ref, sc_ref, w_ref, q_ref, k_ref, v_ref = refs
    h = _adaln_rows(x_ref[...], g_ref[...], sh_ref[...], sc_ref[...]).astype(BF16)
    width = q_ref.shape[1]
    qscale = (HEAD_DIM ** -0.5) * LOG2E
    for part, o_ref in enumerate((q_ref, k_ref, v_ref)):
        r = jnp.dot(h, w_ref[:, part * width:(part + 1) * width], preferred_element_type=F32)
        if part < 2 and rope:
            c = cos_ref[...]
            s = sin_ref[...]
            for hd in range(width // LANES):
                blk = r[:, hd * LANES:(hd + 1) * LANES]
                blk = blk * c + _swap16(blk) * s
                if part == 0:
                    blk = blk * qscale
                o_ref[:, hd * LANES:(hd + 1) * LANES] = blk.astype(o_ref.dtype)
        else:
            if part == 0:
                r = r * qscale
            o_ref[...] = r.astype(o_ref.dtype)


def _qkv_project(x2d, g, shift, scale, w_bf16, rope_tabs, tokens_per_mod, kv_dtype):
    n, d = x2d.shape
    tm = ROW_TILE
    width = w_bf16.shape[1] // 3
    rope = rope_tabs is not None
    blocks_per_mod = tokens_per_mod // tm
    in_specs = [pl.BlockSpec((tm, d), lambda i: (i, 0)),
                pl.BlockSpec((1, d), lambda i: (0, 0)),
                pl.BlockSpec((None, 1, d), lambda i: (i // blocks_per_mod, 0, 0)),
                pl.BlockSpec((None, 1, d), lambda i: (i // blocks_per_mod, 0, 0)),
                pl.BlockSpec((d, 3 * width), lambda i: (0, 0))]
    args = [x2d, g.reshape(1, d), shift, scale, w_bf16]
    if rope:
        cos_t, sin_t = rope_tabs
        seq_blocks = cos_t.shape[0] // tm
        in_specs += [pl.BlockSpec((tm, LANES), lambda i: (i % seq_blocks, 0)),
                     pl.BlockSpec((tm, LANES), lambda i: (i % seq_blocks, 0))]
        args += [cos_t, sin_t]
    out_spec = pl.BlockSpec((tm, width), lambda i: (i, 0))
    return pl.pallas_call(
        functools.partial(_qkv_kernel, rope=rope),
        grid=(n // tm,),
        in_specs=in_specs,
        out_specs=[out_spec, out_spec, out_spec],
        out_shape=[jax.ShapeDtypeStruct((n, width), BF16),
                   jax.ShapeDtypeStruct((n, width), kv_dtype),
                   jax.ShapeDtypeStruct((n, width), kv_dtype)],
        compiler_params=_cparams(("parallel",)),
        name="adaln_qkv_rope" if rope else "adaln_qkv",
    )(*args)


def _attn_kernel(lamv_ref, q_ref, k_ref, v_ref, g_ref, o_ref, qs_ref, m_ref, l_ref, acc_ref,
                 *, tk, lam_init, hp):
    tq = q_ref.shape[0]
    nk = k_ref.shape[0]
    for h in range(hp):
        q = q_ref[:, h * LANES:(h + 1) * LANES]
        lane = lax.broadcasted_iota(jnp.int32, q.shape, 1)
        zero = jnp.zeros_like(q)
        qs_ref[h, 0:tq, :] = jnp.where(lane < HEAD_DIM, q, zero)
        qs_ref[h, tq:2 * tq, :] = jnp.where(lane >= HEAD_DIM, q, zero)
    m_ref[...] = jnp.full(m_ref.shape, -jnp.inf, F32)
    l_ref[...] = jnp.zeros(l_ref.shape, F32)
    acc_ref[...] = jnp.zeros(acc_ref.shape, F32)

    def body(j, carry):
        off = pl.multiple_of(j * tk, tk)
        for h in range(hp):
            kc = k_ref[pl.ds(off, tk), h * LANES:(h + 1) * LANES].astype(BF16)
            vc = v_ref[pl.ds(off, tk), h * LANES:(h + 1) * LANES].astype(BF16)
            s = lax.dot_general(qs_ref[h], kc, (((1,), (1,)), ((), ())),
                                preferred_element_type=F32)
            m_old = m_ref[h]
            m_new = jnp.maximum(m_old, jnp.max(s, axis=1)[:, None])
            alpha = jnp.exp2(m_old - m_new)
            p = jnp.exp2(s - jnp.tile(m_new, (1, tk // LANES)))
            l_ref[h] = alpha * l_ref[h] + jnp.sum(p, axis=1)[:, None]
            acc_ref[h] = alpha * acc_ref[h] + jnp.dot(p.astype(BF16), vc,
                                                      preferred_element_type=F32)
            m_ref[h] = m_new
        return carry

    lax.fori_loop(0, nk // tk, body, 0)

    lv = lamv_ref[...]
    lam = (jnp.exp(jnp.sum(lv[0:1, :] * lv[1:2, :], axis=-1, keepdims=True))
           - jnp.exp(jnp.sum(lv[2:3, :] * lv[3:4, :], axis=-1, keepdims=True)) + lam_init)
    for h in range(hp):
        o = acc_ref[h] / l_ref[h]
        d = o[0:tq, :] - lam * o[tq:2 * tq, :]
        ms = jnp.mean(d * d, axis=-1, keepdims=True)
        o_ref[:, h * LANES:(h + 1) * LANES] = (
            (d * lax.rsqrt(ms + EPS)) * g_ref[...] * (1.0 - lam_init)).astype(o_ref.dtype)


def _diff_attention(q, k, v, lamv, subln_g, lam_init, batch):
    nq = q.shape[0] // batch
    nk = k.shape[0] // batch
    tq = min(ATTN_TQ, nq)
    tk = ATTN_TK if nk % ATTN_TK == 0 else nk
    assert nq % tq == 0 and nk % tk == 0 and tk % LANES == 0
    hp = ATTN_HEADS_PER_STEP
    w = hp * LANES
    q3 = q.reshape(batch, nq, ATTN_WIDTH)
    k3 = k.reshape(batch, nk, ATTN_WIDTH)
    v3 = v.reshape(batch, nk, ATTN_WIDTH)
    out = pl.pallas_call(
        functools.partial(_attn_kernel, tk=tk, lam_init=lam_init, hp=hp),
        grid=(batch, N_HEADS // hp, nq // tq),
        in_specs=[pl.BlockSpec((4, HEAD_DIM), lambda b, h, i: (0, 0)),
                  pl.BlockSpec((None, tq, w), lambda b, h, i: (b, i, h)),
                  pl.BlockSpec((None, nk, w), lambda b, h, i: (b, 0, h)),
                  pl.BlockSpec((None, nk, w), lambda b, h, i: (b, 0, h)),
                  pl.BlockSpec((1, V_DIM), lambda b, h, i: (0, 0))],
        out_specs=pl.BlockSpec((None, tq, w), lambda b, h, i: (b, i, h)),
        out_shape=jax.ShapeDtypeStruct((batch, nq, ATTN_WIDTH), BF16),
        scratch_shapes=[pltpu.VMEM((hp, 2 * tq, V_DIM), BF16),
                        pltpu.VMEM((hp, 2 * tq, LANES), F32),
                        pltpu.VMEM((hp, 2 * tq, LANES), F32),
                        pltpu.VMEM((hp, 2 * tq, V_DIM), F32)],
        compiler_params=_cparams(("parallel", "parallel", "parallel")),
        name="diff_attention",
    )(lamv, q3, k3, v3, subln_g.reshape(1, V_DIM))
    return out.reshape(batch * nq, ATTN_WIDTH)


def _epilogue(x_new, g2, sh2, sc2, wr_t, xo_ref, h2_ref, aff_ref):
    xo_ref[...] = x_new
    h2 = _adaln_rows(x_new, g2, sh2, sc2)
    h2b = h2.astype(BF16)
    for j in range(h2_ref.shape[1]):
        h2_ref[:, j, :] = h2[:, j * LANES:(j + 1) * LANES]
    logits = lax.dot_general(wr_t, h2b, (((1,), (1,)), ((), ())),
                             preferred_element_type=F32)
    mx = jnp.max(logits, axis=0, keepdims=True)
    ex = jnp.exp(logits - mx)
    aff = ex / jnp.sum(ex, axis=0, keepdims=True)
    for r in range(aff_ref.shape[0]):
        aff_ref[r] = aff[:, r * LANES:(r + 1) * LANES]


def _proj_kernel(o_ref, w_ref, x_ref, gate_ref, g2_ref, sh2_ref, sc2_ref, wr_ref,
                 xo_ref, h2_ref, aff_ref):
    y = jnp.dot(o_ref[...], w_ref[...], preferred_element_type=F32)
    x_new = x_ref[...] + gate_ref[...] * y
    _epilogue(x_new, g2_ref[...], sh2_ref[...], sc2_ref[...], wr_ref[...], xo_ref, h2_ref, aff_ref)


def _epilogue_specs(d, tm, blocks_per_mod):
    mod_spec = pl.BlockSpec((None, 1, d), lambda i: (i // blocks_per_mod, 0, 0))
    in_specs = [pl.BlockSpec((1, d), lambda i: (0, 0)), mod_spec, mod_spec,
                pl.BlockSpec((N_EXPERTS, d), lambda i: (0, 0))]
    out_specs = [pl.BlockSpec((tm, d), lambda i: (i, 0)),
                 pl.BlockSpec((tm, d // LANES, LANES), lambda i: (i, 0, 0)),
                 pl.BlockSpec((tm // LANES, N_EXPERTS, LANES), lambda i: (i, 0, 0))]
    return in_specs, out_specs


def _epilogue_out_shape(n, d):
    return [jax.ShapeDtypeStruct((n, d), F32),
            jax.ShapeDtypeStruct((n, d // LANES, LANES), F32),
            jax.ShapeDtypeStruct((n // LANES, N_EXPERTS, LANES), F32)]


def _proj_mixer(o, w_bf16, x2d, gate1, g2, sh2, sc2, wr_t, tokens_per_mod):
    n, d = x2d.shape
    tm = ROW_TILE
    kdim = o.shape[1]
    bpm = tokens_per_mod // tm
    ep_in, ep_out = _epilogue_specs(d, tm, bpm)
    return pl.pallas_call(
        _proj_kernel,
        grid=(n // tm,),
        in_specs=[pl.BlockSpec((tm, kdim), lambda i: (i, 0)),
                  pl.BlockSpec((kdim, d), lambda i: (0, 0)),
                  pl.BlockSpec((tm, d), lambda i: (i, 0)),
                  pl.BlockSpec((None, 1, d), lambda i: (i // bpm, 0, 0))] + ep_in,
        out_specs=ep_out,
        out_shape=_epilogue_out_shape(n, d),
        compiler_params=_cparams(("parallel",)),
        name="proj_mixer",
    )(o, w_bf16, x2d, gate1, g2.reshape(1, d), sh2, sc2, wr_t)


def _pool_kernel(x_ref, xprev_ref, xnext_ref, g1_ref, sh1_ref, sc1_ref, pw_ref, pb_ref, ps_ref,
                 gate_ref, g2_ref, sh2_ref, sc2_ref, wr_ref, xo_ref, h2_ref, aff_ref, *, seq_len):
    tm, d = x_ref.shape
    halo = POOL_HALO
    i = pl.program_id(0)
    pos0 = (i * tm) % seq_len
    g1, sh1, sc1 = g1_ref[...], sh1_ref[...], sc1_ref[...]
    x = x_ref[...]
    h = _adaln_rows(x, g1, sh1, sc1)
    hprev = _adaln_rows(xprev_ref[...], g1, sh1, sc1)
    hnext = _adaln_rows(xnext_ref[...], g1, sh1, sc1)
    hprev = jnp.where(pos0 > 0, hprev, 0.0)
    hnext = jnp.where(pos0 + tm < seq_len, hnext, 0.0)
    hext = jnp.concatenate([hprev, h, hnext], axis=0)
    rows = tm + 2 * halo
    t = pos0 + lax.broadcasted_iota(jnp.int32, (tm, 1), 0)
    outs = []
    for gi, win in enumerate(POOL_WINDOWS):
        a = hext[:, gi * POOL_GROUP:(gi + 1) * POOL_GROUP]
        p = pltpu.roll(a, 1, axis=0) + a
        step = 1
        while 2 * step < win:
            p = pltpu.roll(p, step, axis=0) + pltpu.roll(p, rows - step, axis=0)
            step *= 2
        half = win // 2
        cnt = (jnp.minimum(t + half, seq_len) - jnp.maximum(t - half, 0)).astype(F32)
        pooled = p[halo:halo + tm, :] / cnt - a[halo:halo + tm, :]
        y = jnp.dot(pooled.astype(BF16), pw_ref[gi].astype(BF16), preferred_element_type=F32)
        outs.append(y + pb_ref[gi:gi + 1, :])
    mix = jnp.concatenate(outs, axis=1) * ps_ref[...]
    x_new = x + gate_ref[...] * mix
    _epilogue(x_new, g2_ref[...], sh2_ref[...], sc2_ref[...], wr_ref[...], xo_ref, h2_ref, aff_ref)


def _pool_mixer(x2d, g1, sh1, sc1, pool_w, pool_b, pool_scale, gate1, g2, sh2, sc2, wr_t,
                tokens_per_mod, seq_len):
    n, d = x2d.shape
    tm = min(ROW_TILE, seq_len)
    bpm = tokens_per_mod // tm
    hb = tm // POOL_HALO
    n_hblocks = n // POOL_HALO
    mod_spec = pl.BlockSpec((None, 1, d), lambda i: (i // bpm, 0, 0))
    ep_in, ep_out = _epilogue_specs(d, tm, bpm)
    return pl.pallas_call(
        functools.partial(_pool_kernel, seq_len=seq_len),
        grid=(n // tm,),
        in_specs=[pl.BlockSpec((tm, d), lambda i: (i, 0)),
                  pl.BlockSpec((POOL_HALO, d), lambda i: (jnp.maximum(i * hb - 1, 0), 0)),
                  pl.BlockSpec((POOL_HALO, d),
                               lambda i: (jnp.minimum((i + 1) * hb, n_hblocks - 1), 0)),
                  pl.BlockSpec((1, d), lambda i: (0, 0)), mod_spec, mod_spec,
                  pl.BlockSpec((N_POOL_GROUPS, POOL_GROUP, POOL_GROUP), lambda i: (0, 0, 0)),
                  pl.BlockSpec((N_POOL_GROUPS, POOL_GROUP), lambda i: (0, 0)),
                  pl.BlockSpec((1, d), lambda i: (0, 0)),
                  mod_spec] + ep_in,
        out_specs=ep_out,
        out_shape=_epilogue_out_shape(n, d),
        compiler_params=_cparams(("parallel",)),
        name="pool_mixer",
    )(x2d, x2d, x2d, g1.reshape(1, d), sh1, sc1, pool_w, pool_b, pool_scale.reshape(1, d),
      gate1, g2.reshape(1, d), sh2, sc2, wr_t)


def _ffn_gather_copy(h2p_hbm, xbuf, sem, slot, src_row, dst_row, nrows):
    return pltpu.make_async_copy(h2p_hbm.at[pl.ds(src_row, nrows)],
                                 xbuf.at[slot, pl.ds(dst_row, nrows)], sem.at[slot])


def _ffn_kernel(idx_ref, h2p_hbm, wg_ref, wu_ref, wd_ref, g_ref, ye_ref, xbuf, sem,
                *, rows, n_tokens):
    n_e, n_s = pl.num_programs(0), pl.num_programs(1)
    ei, si = pl.program_id(0), pl.program_id(1)
    cap, pieces = xbuf.shape[1], xbuf.shape[2]
    step = ei * n_s + si
    slot = step % 2

    def issue(e_t, s_t, slot_t):
        base = (s_t * n_e + e_t) * cap
        row0 = s_t * n_tokens

        def body(kb, carry):
            k0 = kb * GATHER_UNROLL
            for u in range(GATHER_UNROLL):
                _ffn_gather_copy(h2p_hbm, xbuf, sem, slot_t, row0 + idx_ref[base + k0 + u],
                                 k0 + u, 1).start()
            return carry

        lax.fori_loop(0, cap // GATHER_UNROLL, body, 0)

    @pl.when(step == 0)
    def _():
        issue(ei, si, slot)

    @pl.when(step + 1 < n_e * n_s)
    def _():
        nxt = step + 1
        issue(nxt // n_s, nxt % n_s, 1 - slot)

    _ffn_gather_copy(h2p_hbm, xbuf, sem, slot, 0, 0, cap).wait()

    def body(i, carry):
        r0 = pl.multiple_of(i * rows, rows)
        x = jnp.concatenate([xbuf[slot, pl.ds(r0, rows), j, :].astype(BF16)
                             for j in range(pieces)], axis=1)
        a = jnp.dot(x, wg_ref[...], preferred_element_type=F32)
        b = jnp.dot(x, wu_ref[...], preferred_element_type=F32)
        hid = (a * (1.0 / (1.0 + jnp.exp(-a))) * b).astype(BF16)
        y = jnp.dot(hid, wd_ref[...], preferred_element_type=F32)
        gate = jnp.tile(g_ref[pl.ds(r0, rows), :], (1, y.shape[1] // LANES))
        ye_ref[pl.ds(r0, rows), :] = (y * gate).astype(ye_ref.dtype)
        return carry

    lax.fori_loop(0, cap // rows, body, 0)


def _expert_ffn(idx_flat, h2p_all, wg, wu, wd, gates, n_tokens, layer):
    s, e, c, _ = gates.shape
    d, f = wg.shape[2], wg.shape[3]
    return pl.pallas_call(
        functools.partial(_ffn_kernel, rows=min(FFN_ROWS, c), n_tokens=n_tokens),
        grid_spec=pltpu.PrefetchScalarGridSpec(
            num_scalar_prefetch=1,
            grid=(e, s),
            in_specs=[pl.BlockSpec(memory_space=pl.ANY),
                      pl.BlockSpec((None, None, d, f), lambda ei, si, idx: (layer, ei, 0, 0)),
                      pl.BlockSpec((None, None, d, f), lambda ei, si, idx: (layer, ei, 0, 0)),
                      pl.BlockSpec((None, None, f, d), lambda ei, si, idx: (layer, ei, 0, 0)),
                      pl.BlockSpec((None, None, c, LANES), lambda ei, si, idx: (si, ei, 0, 0))],
            out_specs=pl.BlockSpec((None, None, c, d), lambda ei, si, idx: (si, ei, 0, 0)),
            scratch_shapes=[pltpu.VMEM((2, c, d // LANES, LANES), F32),
                            pltpu.SemaphoreType.DMA((2,))]),
        out_shape=jax.ShapeDtypeStruct((s, e, c, d), BF16),
        compiler_params=pltpu.CompilerParams(dimension_semantics=("arbitrary", "arbitrary"),
                                             vmem_limit_bytes=VMEM_LIMIT,
                                             disable_bounds_checks=True),
        name="expert_ffn",
    )(idx_flat, h2p_all, wg, wu, wd, gates)


def _combine_kernel(clo_ref, ncnt_ref, ye_hbm, tok_hbm, x_ref, gate_ref, fg_ref, o_ref,
                    zbuf, tokbuf, acc_ref, sem, *, final):
    n_e = ye_hbm.shape[0]
    tm, d = x_ref.shape
    ch, grp = COMBINE_CH, COMBINE_GROUP
    i = pl.program_id(0)
    nt = pl.num_programs(0)
    slot = i % 2

    def chunk_copies(e, src_row, slot_t, dst_row):
        return (pltpu.make_async_copy(ye_hbm.at[e, pl.ds(src_row, ch), :],
                                      zbuf.at[slot_t, pl.ds(dst_row, ch), :], sem.at[slot_t]),
                pltpu.make_async_copy(tok_hbm.at[e, pl.ds(src_row, ch), :],
                                      tokbuf.at[slot_t, pl.ds(dst_row, ch), :], sem.at[slot_t]))

    def issue(tile, slot_t):
        pos = 0
        for e in range(n_e):
            lo = clo_ref[tile * n_e + e]

            def body(j, p, e=e, lo=lo):
                src = pl.multiple_of((lo + j) * ch, ch)
                dst = pl.multiple_of(p * ch, ch)
                for cp in chunk_copies(e, src, slot_t, dst):
                    cp.start()
                return p + 1

            pos = lax.fori_loop(0, ncnt_ref[tile * n_e + e], body, pos)

    def total_chunks(tile):
        tot = 0
        for e in range(n_e):
            tot = tot + ncnt_ref[tile * n_e + e]
        return tot

    @pl.when(i == 0)
    def _():
        zbuf[...] = jnp.zeros(zbuf.shape, zbuf.dtype)
        tokbuf[...] = jnp.full(tokbuf.shape, -1, jnp.int32)
        issue(i, slot)

    @pl.when(i + 1 < nt)
    def _():
        issue(i + 1, 1 - slot)

    total = total_chunks(i)

    def wait_body(j, carry):
        for cp in chunk_copies(0, 0, slot, 0):
            cp.wait()
        return carry

    lax.fori_loop(0, total, wait_body, 0)

    acc_ref[...] = jnp.zeros(acc_ref.shape, F32)
    tok_of_lane = i * tm + lax.broadcasted_iota(jnp.int32, (grp, tm), 1)
    row_in_group = lax.broadcasted_iota(jnp.int32, (grp, 1), 0)

    def group_body(gidx, carry):
        r0 = pl.multiple_of(gidx * grp, grp)
        tok = jnp.where(r0 + row_in_group < total * ch, tokbuf[slot, pl.ds(r0, grp), :], -1)
        tok = jnp.tile(tok, (1, tm // LANES))
        onehot_t = (tok == tok_of_lane).astype(BF16)
        acc_ref[...] += lax.dot_general(onehot_t, zbuf[slot, pl.ds(r0, grp), :],
                                        (((0,), (0,)), ((), ())), preferred_element_type=F32)
        return carry

    lax.fori_loop(0, (total * ch + grp - 1) // grp, group_body, 0)

    x = x_ref[...] + gate_ref[...] * acc_ref[...]
    if final:
        ms = jnp.mean(x * x, axis=-1, keepdims=True)
        x = x * lax.rsqrt(ms + EPS) * fg_ref[...]
    o_ref[...] = x


def _combine(ye, tok_col, toff, x2d, gate2, final_g, tokens_per_mod, final):
    n, d = x2d.shape
    e, c, _ = ye.shape
    tm, ch = COMBINE_TM, COMBINE_CH
    n_tiles = n // tm
    bpm = tokens_per_mod // tm
    start = toff[::tm // LANES, :, 0]
    first = jnp.concatenate([start, jnp.full((1, e), c, jnp.int32)], axis=0)
    clo = first[:-1] // ch
    chi = (first[1:] + ch - 1) // ch
    ncnt = jnp.maximum(chi - clo, 0)
    max_rows = e * (tm // ch + 2) * ch
    max_rows = ((max_rows + COMBINE_GROUP - 1) // COMBINE_GROUP) * COMBINE_GROUP
    return pl.pallas_call(
        functools.partial(_combine_kernel, final=final),
        grid_spec=pltpu.PrefetchScalarGridSpec(
            num_scalar_prefetch=2,
            grid=(n_tiles,),
            in_specs=[pl.BlockSpec(memory_space=pl.ANY),
                      pl.BlockSpec(memory_space=pl.ANY),
                      pl.BlockSpec((tm, d), lambda i, a, b: (i, 0)),
                      pl.BlockSpec((None, 1, d), lambda i, a, b: (i // bpm, 0, 0)),
                      pl.BlockSpec((1, d), lambda i, a, b: (0, 0))],
            out_specs=pl.BlockSpec((tm, d), lambda i, a, b: (i, 0)),
            scratch_shapes=[pltpu.VMEM((2, max_rows, d), BF16),
                            pltpu.VMEM((2, max_rows, LANES), jnp.int32),
                            pltpu.VMEM((tm, d), F32),
                            pltpu.SemaphoreType.DMA((2,))]),
        out_shape=jax.ShapeDtypeStruct((n, d), F32),
        compiler_params=_cparams(("arbitrary",)),
        name="combine_final" if final else "combine",
    )(clo.reshape(-1), ncnt.reshape(-1), ye, tok_col, x2d, gate2, final_g.reshape(1, d))


def _rope_tables(n_tokens):
    t = jnp.arange(n_tokens)
    row = (t // GRID_W).astype(F32)
    col = (t % GRID_W).astype(F32)
    half = HEAD_DIM // 2
    inv = ROPE_BASE ** (-jnp.arange(0, half, 2, dtype=F32) / half)
    ar, ac = row[:, None] * inv, col[:, None] * inv
    c64 = jnp.concatenate([jnp.cos(ar), jnp.cos(ar), jnp.cos(ac), jnp.cos(ac)], axis=-1)
    s64 = jnp.concatenate([-jnp.sin(ar), jnp.sin(ar), -jnp.sin(ac), jnp.sin(ac)], axis=-1)
    return jnp.tile(c64, (1, 2)), jnp.tile(s64, (1, 2))


def _tile_prefix(mask3, il_ref, to_ref, tot_ref):
    nt, ne, ln = mask3.shape
    m2 = mask3.astype(F32).astype(BF16).reshape(nt * ne, ln)
    r = lax.broadcasted_iota(jnp.int32, (ln, ln), 0)
    c = lax.broadcasted_iota(jnp.int32, (ln, ln), 1)
    upper = (r <= c).astype(F32).astype(BF16)
    ones = jnp.ones((ln, ln), BF16)
    il_ref[...] = jnp.dot(m2, upper, preferred_element_type=F32).reshape(nt, ne, ln)
    tot_ref[...] = jnp.dot(m2, ones, preferred_element_type=F32).reshape(nt, ne, ln)
    run = jnp.zeros((ne, ln), F32)
    for tt in range(nt):
        to_ref[tt] = run
        run = run + tot_ref[tt]


def _split3_bf16(v):
    p1 = v.astype(BF16)
    r1 = v - p1.astype(F32)
    p2 = r1.astype(BF16)
    return p1, p2, (r1 - p2.astype(F32)).astype(BF16)


def _route_kernel(aff_ref, idx_ref, gcol_ref, tokcol_ref, toff_ref, il_ref, to_ref, tot_ref, *, cap):
    nt, ne, ln = aff_ref.shape
    a3 = aff_ref[...]

    def count(mask3):
        per_lane = jnp.sum(mask3.astype(jnp.int32), axis=0)
        return jnp.sum(per_lane, axis=1, keepdims=True)

    def as_float(thr_bits):
        return pltpu.bitcast(jnp.broadcast_to(thr_bits, (ne, ln)), F32)[:, 0:1]

    def search(i, thr_bits):
        cand = thr_bits | lax.shift_left(jnp.int32(1), 30 - i)
        return jnp.where(count(a3 >= as_float(cand)[None]) >= cap, cand, thr_bits)

    thr = as_float(lax.fori_loop(0, 31, search, jnp.zeros((ne, 1), jnp.int32)))
    gt = a3 > thr[None]
    eq = a3 == thr[None]
    need = (cap - count(gt)).astype(F32)
    _tile_prefix(eq, il_ref, to_ref, tot_ref)
    eq_before = to_ref[...] + il_ref[...] - eq.astype(F32)
    sel = gt | (eq & (eq_before < need[None]))
    _tile_prefix(sel, il_ref, to_ref, tot_ref)
    toff_ref[...] = to_ref[...].astype(jnp.int32)

    reps = cap // ln
    ones = jnp.ones((ln, ln), BF16)

    def to_column(vals, bf16_exact=False):
        out = None
        for piece in ((vals.astype(BF16),) if bf16_exact else _split3_bf16(vals)):
            t = lax.dot_general(piece, ones, (((0,), (0,)), ((), ())), preferred_element_type=F32)
            out = t if out is None else out + t
        return out

    slot = lax.broadcasted_iota(jnp.int32, (nt, cap), 1).astype(F32)
    tile_id = lax.broadcasted_iota(jnp.int32, (nt, cap), 0).astype(F32)
    in_tile = lax.broadcasted_iota(jnp.int32, (ln, cap), 0).astype(F32)
    for e in range(ne):
        il_e = il_ref[:, e, :]
        to_e = jnp.tile(to_ref[:, e, :], (1, reps))
        tot_e = jnp.tile(tot_ref[:, e, :], (1, reps))
        owns = (to_e <= slot) & (slot < to_e + tot_e)
        rank = jnp.sum(jnp.where(owns, slot - to_e, 0.0), axis=0, keepdims=True)
        tile_of = jnp.sum(jnp.where(owns, tile_id, 0.0), axis=0, keepdims=True)
        owns_b = owns.astype(F32).astype(BF16)
        contract0 = (((0,), (0,)), ((), ()))
        il_g = lax.dot_general(il_e.astype(BF16), owns_b, contract0,
                               preferred_element_type=F32)
        pos = jnp.sum((il_g <= rank).astype(F32), axis=0, keepdims=True)
        idx_ref[e:e + 1, :] = (tile_of * ln + pos).astype(jnp.int32)
        p1, p2, p3 = _split3_bf16(aff_ref[:, e, :])
        a_g = (lax.dot_general(p1, owns_b, contract0, preferred_element_type=F32)
               + lax.dot_general(p2, owns_b, contract0, preferred_element_type=F32)
               + lax.dot_general(p3, owns_b, contract0, preferred_element_type=F32))
        hit = in_tile == pos
        gcol_ref[e] = to_column(jnp.where(hit, a_g, 0.0))
        tok_hi = to_column(jnp.where(hit, tile_of, 0.0), bf16_exact=True)
        tok_lo = to_column(jnp.where(hit, pos, 0.0), bf16_exact=True)
        tokcol_ref[e] = (tok_hi * ln + tok_lo).astype(jnp.int32)


def _route(aff_all, cap):
    s, nt, ne, ln = aff_all.shape
    assert nt <= 256 and cap % ln == 0
    return pl.pallas_call(
        functools.partial(_route_kernel, cap=cap),
        grid=(s,),
        in_specs=[pl.BlockSpec((None, nt, ne, ln), lambda i: (i, 0, 0, 0))],
        out_specs=[pl.BlockSpec((None, ne, cap), lambda i: (i, 0, 0)),
                   pl.BlockSpec((None, ne, cap, ln), lambda i: (i, 0, 0, 0)),
                   pl.BlockSpec((None, ne, cap, ln), lambda i: (i, 0, 0, 0)),
                   pl.BlockSpec((None, nt, ne, ln), lambda i: (i, 0, 0, 0))],
        out_shape=[jax.ShapeDtypeStruct((s, ne, cap), jnp.int32),
                   jax.ShapeDtypeStruct((s, ne, cap, ln), F32),
                   jax.ShapeDtypeStruct((s, ne, cap, ln), jnp.int32),
                   jax.ShapeDtypeStruct((s, nt, ne, ln), jnp.int32)],
        scratch_shapes=[pltpu.VMEM((nt, ne, ln), F32)] * 3,
        compiler_params=_cparams(("parallel",)),
        name="route",
    )(aff_all)


def _route_and_ffn(h2_list, aff_list, w_gate, w_up, w_down, layer):
    n = h2_list[0].shape[0]
    assert all(h.shape[0] == n for h in h2_list)
    cap = EC_CAPACITY * n // N_EXPERTS
    idx_all, gcol, tokcol, toff = _route(jnp.stack(aff_list), cap)
    ye = _expert_ffn(idx_all.reshape(-1), jnp.concatenate(h2_list, axis=0), w_gate, w_up, w_down,
                     gcol, n, layer)
    return ye, tokcol, toff


def kernel(x_prompt, x_sample, cache_k, cache_v, c, c_ctx, w_mod, b_mod, norm1_g, norm2_g,
           attn_wqkv, attn_wo, lambda_q1, lambda_k1, lambda_q2, lambda_k2, subln_g,
           pool_w, pool_b, pool_scale, router_w, expert_w_gate, expert_w_up, expert_w_down,
           final_g):
    bp, n_ctx, d = x_prompt.shape
    bs, n_lat, _ = x_sample.shape
    past = cache_k.shape[2]
    n_p, n_s = bp * n_ctx, bs * n_lat

    rows = 8 * ((1 + bs + 7) // 8)
    cvecs = jnp.zeros((rows, d), F32).at[0].set(c_ctx).at[1:1 + bs].set(c)
    mod = _modulation_all(cvecs, w_mod, b_mod)
    rope_tabs = _rope_tables(n_lat)
    w_gate_b = expert_w_gate.astype(BF16)
    w_up_b = expert_w_up.astype(BF16)
    w_down_b = expert_w_down.astype(BF16)

    xp = x_prompt.reshape(n_p, d)
    xs = x_sample.reshape(n_s, d)
    new_k, new_v = [], []
    for l in range(DEPTH):
        mp_ = mod[l, 0:1].reshape(1, 6, 1, d)
        ms_ = mod[l, 1:1 + bs].reshape(bs, 6, 1, d)
        sp1, cp1, gp1, sp2, cp2, gp2 = [mp_[:, i] for i in range(6)]
        ss1, cs1, gs1, ss2, cs2, gs2 = [ms_[:, i] for i in range(6)]
        wr_t = router_w[l].T.astype(BF16)
        if l % 2 == 0:
            a = l // 2
            lam_init = 0.8 - 0.6 * math.exp(-0.3 * l)
            lamv = jnp.stack([lambda_q1[a], lambda_k1[a], lambda_q2[a], lambda_k2[a]])
            wqkv = attn_wqkv[a].astype(BF16)
            wo = attn_wo[a].astype(BF16)
            qp, kp, vp = _qkv_project(xp, norm1_g[l], sp1, cp1, wqkv, None, n_p, F32)
            new_k.append(kp.reshape(bp, n_ctx, N_HEADS, 2 * HEAD_DIM))
            new_v.append(vp.reshape(bp, n_ctx, N_HEADS, V_DIM))
            op = _diff_attention(qp, kp, vp, lamv, subln_g[a], lam_init, bp)
            xp, h2p, affp = _proj_mixer(op, wo, xp, gp1, norm2_g[l], sp2, cp2, wr_t, n_p)
            qs, ks_, vs = _qkv_project(xs, norm1_g[l], ss1, cs1, wqkv, rope_tabs, n_lat, BF16)
            k_all = jnp.concatenate([ks_.reshape(bs, n_lat, ATTN_WIDTH),
                                     cache_k[:, a].reshape(bs, past, ATTN_WIDTH).astype(BF16)],
                                    axis=1).reshape(bs * (n_lat + past), ATTN_WIDTH)
            v_all = jnp.concatenate([vs.reshape(bs, n_lat, ATTN_WIDTH),
                                     cache_v[:, a].reshape(bs, past, ATTN_WIDTH).astype(BF16)],
                                    axis=1).reshape(bs * (n_lat + past), ATTN_WIDTH)
            os_ = _diff_attention(qs, k_all, v_all, lamv, subln_g[a], lam_init, bs)
            xs, h2s, affs = _proj_mixer(os_, wo, xs, gs1, norm2_g[l], ss2, cs2, wr_t, n_lat)
        else:
            p = l // 2
            xp, h2p, affp = _pool_mixer(xp, norm1_g[l], sp1, cp1, pool_w[p], pool_b[p],
                                        pool_scale[p], gp1, norm2_g[l], sp2, cp2, wr_t,
                                        n_p, n_ctx)
            xs, h2s, affs = _pool_mixer(xs, norm1_g[l], ss1, cs1, pool_w[p], pool_b[p],
                                        pool_scale[p], gs1, norm2_g[l], ss2, cs2, wr_t,
                                        n_lat, n_lat)
        ye, tokcol, toff = _route_and_ffn([h2p, h2s], [affp, affs], w_gate_b, w_up_b, w_down_b, l)
        last = l == DEPTH - 1
        xp = _combine(ye[0], tokcol[0], toff[0], xp, gp2, final_g, n_p, last)
        xs = _combine(ye[1], tokcol[1], toff[1], xs, gs2, final_g, n_lat, last)
    return (xp.reshape(bp, n_ctx, d), xs.reshape(bs, n_lat, d),
            jnp.stack(new_k, axis=1), jnp.stack(new_v, axis=1))
```

```python
import math
import functools
import jax
import jax.numpy as jnp
from jax import lax
from jax.experimental import pallas as pl
from jax.experimental.pallas import tpu as pltpu

D_MODEL = 1024
DEPTH = 4
GRID_W = 64
N_HEADS = 8
HEAD_DIM = 64
V_DIM = 2 * HEAD_DIM
ATTN_WIDTH = N_HEADS * V_DIM
ROPE_BASE = 10000.0
N_POOL_GROUPS = 4
POOL_GROUP = D_MODEL // N_POOL_GROUPS
POOL_WINDOWS = (2, 4, 8, 16)
POOL_HALO = 8
N_EXPERTS = 16
EC_CAPACITY = 2
EPS = 1e-6

F32 = jnp.float32
BF16 = jnp.bfloat16
LANES = 128
LOG2E = 1.4426950408889634
VMEM_LIMIT = 56 * 1024 * 1024
ROW_TILE = 512
ATTN_TQ = 256
ATTN_TK = 1024
ATTN_HEADS_PER_STEP = 2
FFN_ROWS = 512
GATHER_UNROLL = 8
COMBINE_TM = 256
COMBINE_CH = 32
COMBINE_GROUP = 256


def _cparams(sem):
    return pltpu.CompilerParams(dimension_semantics=sem, vmem_limit_bytes=VMEM_LIMIT)


def _mod_kernel(c_ref, w_ref, b_ref, o_ref):
    c = c_ref[...]
    s = (c * (1.0 / (1.0 + jnp.exp(-c)))).astype(BF16)
    o_ref[...] = jnp.dot(s, w_ref[...].astype(BF16), preferred_element_type=F32) + b_ref[...]


def _modulation_all(cvecs, w_mod, b_mod):
    r, d = cvecs.shape
    depth, _, n6 = w_mod.shape
    tn = 1024
    return pl.pallas_call(
        _mod_kernel,
        grid=(depth, n6 // tn),
        in_specs=[pl.BlockSpec((r, d), lambda l, j: (0, 0)),
                  pl.BlockSpec((None, d, tn), lambda l, j: (l, 0, j)),
                  pl.BlockSpec((None, 1, tn), lambda l, j: (l, 0, j))],
        out_specs=pl.BlockSpec((None, r, tn), lambda l, j: (l, 0, j)),
        out_shape=jax.ShapeDtypeStruct((depth, r, n6), F32),
        compiler_params=_cparams(("parallel", "parallel")),
        name="modulation",
    )(cvecs, w_mod, b_mod.reshape(depth, 1, n6))


def _adaln_rows(x, g, shift, scale):
    ms = jnp.mean(x * x, axis=-1, keepdims=True)
    return (x * lax.rsqrt(ms + EPS)) * g * (1.0 + scale) + shift


def _swap16(x):
    lane = lax.broadcasted_iota(jnp.int32, x.shape, 1)
    up = pltpu.roll(x, x.shape[1] - 16, axis=1)
    dn = pltpu.roll(x, 16, axis=1)
    return jnp.where((lane % 32) < 16, up, dn)


def _qkv_kernel(*refs, rope):
    if rope:
        x_ref, g_ref, sh_ref, sc_ref, w_ref, cos_ref, sin_ref, q_ref, k_ref, v_ref = refs
    else:
        x_ref, g_ref, sh_ref, sc_ref, w_ref, q_ref, k_ref, v_ref = refs
    h = _adaln_rows(x_ref[...], g_ref[...], sh_ref[...], sc_ref[...]).astype(BF16)
    width = q_ref.shape[1]
    qscale = (HEAD_DIM ** -0.5) * LOG2E
    for part, o_ref in enumerate((q_ref, k_ref, v_ref)):
        r = jnp.dot(h, w_ref[:, part * width:(part + 1) * width], preferred_element_type=F32)
        if part < 2 and rope:
            c = cos_ref[...]
            s = sin_ref[...]
            for hd in range(width // LANES):
                blk = r[:, hd * LANES:(hd + 1) * LANES]
                blk = blk * c + _swap16(blk) * s
                if part == 0:
                    blk = blk * qscale
                o_ref[:, hd * LANES:(hd + 1) * LANES] = blk.astype(o_ref.dtype)
        else:
            if part == 0:
                r = r * qscale
            o_ref[...] = r.astype(o_ref.dtype)


def _qkv_project(x2d, g, shift, scale, w_bf16, rope_tabs, tokens_per_mod, kv_dtype):
    n, d = x2d.shape
    tm = ROW_TILE
    width = w_bf16.shape[1] // 3
    rope = rope_tabs is not None
    blocks_per_mod = tokens_per_mod // tm
    in_specs = [pl.BlockSpec((tm, d), lambda i: (i, 0)),
                pl.BlockSpec((1, d), lambda i: (0, 0)),
                pl.BlockSpec((None, 1, d), lambda i: (i // blocks_per_mod, 0, 0)),
                pl.BlockSpec((None, 1, d), lambda i: (i // blocks_per_mod, 0, 0)),
                pl.BlockSpec((d, 3 * width), lambda i: (0, 0))]
    args = [x2d, g.reshape(1, d), shift, scale, w_bf16]
    if rope:
        cos_t, sin_t = rope_tabs
        seq_blocks = cos_t.shape[0] // tm
        in_specs += [pl.BlockSpec((tm, LANES), lambda i: (i % seq_blocks, 0)),
                     pl.BlockSpec((tm, LANES), lambda i: (i % seq_blocks, 0))]
        args += [cos_t, sin_t]
    out_spec = pl.BlockSpec((tm, width), lambda i: (i, 0))
    return pl.pallas_call(
        functools.partial(_qkv_kernel, rope=rope),
        grid=(n // tm,),
        in_specs=in_specs,
        out_specs=[out_spec, out_spec, out_spec],
        out_shape=[jax.ShapeDtypeStruct((n, width), BF16),
                   jax.ShapeDtypeStruct((n, width), kv_dtype),
                   jax.ShapeDtypeStruct((n, width), kv_dtype)],
        compiler_params=_cparams(("parallel",)),
        name="adaln_qkv_rope" if rope else "adaln_qkv",
    )(*args)


def _attn_kernel(*refs, tk, lam_init, hp, cached):
    if cached:
        (lamv_ref, q_ref, k_ref, v_ref, kc_ref, vc_ref, g_ref, o_ref,
         qs_ref, m_ref, l_ref, acc_ref) = refs
    else:
        lamv_ref, q_ref, k_ref, v_ref, g_ref, o_ref, qs_ref, m_ref, l_ref, acc_ref = refs
    tq = q_ref.shape[0]
    nk = k_ref.shape[0]
    for h in range(hp):
        q = q_ref[:, h * LANES:(h + 1) * LANES]
        lane = lax.broadcasted_iota(jnp.int32, q.shape, 1)
        zero = jnp.zeros_like(q)
        qs_ref[h, 0:tq, :] = jnp.where(lane < HEAD_DIM, q, zero)
        qs_ref[h, tq:2 * tq, :] = jnp.where(lane >= HEAD_DIM, q, zero)
    m_ref[...] = jnp.full(m_ref.shape, -jnp.inf, F32)
    l_ref[...] = jnp.zeros(l_ref.shape, F32)
    acc_ref[...] = jnp.zeros(acc_ref.shape, F32)

    def online_softmax_step(h, kc, vc):
        s = lax.dot_general(qs_ref[h], kc, (((1,), (1,)), ((), ())),
                            preferred_element_type=F32)
        m_old = m_ref[h]
        m_new = jnp.maximum(m_old, jnp.max(s, axis=1)[:, None])
        alpha = jnp.exp2(m_old - m_new)
        p = jnp.exp2(s - jnp.tile(m_new, (1, s.shape[1] // LANES)))
        l_ref[h] = alpha * l_ref[h] + jnp.sum(p, axis=1)[:, None]
        acc_ref[h] = alpha * acc_ref[h] + jnp.dot(p.astype(BF16), vc, preferred_element_type=F32)
        m_ref[h] = m_new

    def body(j, carry):
        off = pl.multiple_of(j * tk, tk)
        for h in range(hp):
            online_softmax_step(h, k_ref[pl.ds(off, tk), h * LANES:(h + 1) * LANES].astype(BF16),
                                v_ref[pl.ds(off, tk), h * LANES:(h + 1) * LANES].astype(BF16))
        return carry

    lax.fori_loop(0, nk // tk, body, 0)
    if cached:
        for h in range(hp):
            online_softmax_step(h, kc_ref[:, h * LANES:(h + 1) * LANES].astype(BF16),
                                vc_ref[:, h * LANES:(h + 1) * LANES].astype(BF16))

    lv = lamv_ref[...]
    lam = (jnp.exp(jnp.sum(lv[0:1, :] * lv[1:2, :], axis=-1, keepdims=True))
           - jnp.exp(jnp.sum(lv[2:3, :] * lv[3:4, :], axis=-1, keepdims=True)) + lam_init)
    for h in range(hp):
        o = acc_ref[h] / l_ref[h]
        d = o[0:tq, :] - lam * o[tq:2 * tq, :]
        ms = jnp.mean(d * d, axis=-1, keepdims=True)
        o_ref[:, h * LANES:(h + 1) * LANES] = (
            (d * lax.rsqrt(ms + EPS)) * g_ref[...] * (1.0 - lam_init)).astype(o_ref.dtype)


def _diff_attention(q, k, v, lamv, subln_g, lam_init, batch, cache=None):
    nq = q.shape[0] // batch
    nk = k.shape[0] // batch
    tq = min(ATTN_TQ, nq)
    tk = min(ATTN_TK, nk)
    assert nq % tq == 0 and nk % tk == 0 and tk % LANES == 0
    hp = ATTN_HEADS_PER_STEP
    w = hp * LANES
    q3 = q.reshape(batch, nq, ATTN_WIDTH)
    k3 = k.reshape(batch, nk, ATTN_WIDTH)
    v3 = v.reshape(batch, nk, ATTN_WIDTH)
    kv_spec = pl.BlockSpec((None, nk, w), lambda b, h, i: (b, 0, h))
    in_specs = [pl.BlockSpec((4, HEAD_DIM), lambda b, h, i: (0, 0)),
                pl.BlockSpec((None, tq, w), lambda b, h, i: (b, i, h)), kv_spec, kv_spec]
    args = [lamv, q3, k3, v3]
    if cache is not None:
        ck, cv, layer = cache
        past = ck.shape[2]
        assert past % LANES == 0
        c_spec = pl.BlockSpec((None, None, past, w), lambda b, h, i: (b, layer, 0, h))
        in_specs += [c_spec, c_spec]
        args += [ck, cv]
    in_specs.append(pl.BlockSpec((1, V_DIM), lambda b, h, i: (0, 0)))
    args.append(subln_g.reshape(1, V_DIM))
    out = pl.pallas_call(
        functools.partial(_attn_kernel, tk=tk, lam_init=lam_init, hp=hp, cached=cache is not None),
        grid=(batch, N_HEADS // hp, nq // tq),
        in_specs=in_specs,
        out_specs=pl.BlockSpec((None, tq, w), lambda b, h, i: (b, i, h)),
        out_shape=jax.ShapeDtypeStruct((batch, nq, ATTN_WIDTH), BF16),
        scratch_shapes=[pltpu.VMEM((hp, 2 * tq, V_DIM), BF16),
                        pltpu.VMEM((hp, 2 * tq, LANES), F32),
                        pltpu.VMEM((hp, 2 * tq, LANES), F32),
                        pltpu.VMEM((hp, 2 * tq, V_DIM), F32)],
        compiler_params=_cparams(("parallel", "parallel", "parallel")),
        name="diff_attention",
    )(*args)
    return out.reshape(batch * nq, ATTN_WIDTH)


def _epilogue(x_new, g2, sh2, sc2, wr_t, xo_ref, h2_ref, aff_ref):
    xo_ref[...] = x_new
    h2 = _adaln_rows(x_new, g2, sh2, sc2)
    h2b = h2.astype(BF16)
    for j in range(h2_ref.shape[1]):
        h2_ref[:, j, :] = h2[:, j * LANES:(j + 1) * LANES]
    logits = lax.dot_general(wr_t, h2b, (((1,), (1,)), ((), ())),
                             preferred_element_type=F32)
    mx = jnp.max(logits, axis=0, keepdims=True)
    ex = jnp.exp(logits - mx)
    aff = ex / jnp.sum(ex, axis=0, keepdims=True)
    for r in range(aff_ref.shape[0]):
        aff_ref[r] = aff[:, r * LANES:(r + 1) * LANES]


def _proj_kernel(o_ref, w_ref, x_ref, gate_ref, g2_ref, sh2_ref, sc2_ref, wr_ref,
                 xo_ref, h2_ref, aff_ref):
    y = jnp.dot(o_ref[...], w_ref[...], preferred_element_type=F32)
    x_new = x_ref[...] + gate_ref[...] * y
    _epilogue(x_new, g2_ref[...], sh2_ref[...], sc2_ref[...], wr_ref[...], xo_ref, h2_ref, aff_ref)


def _epilogue_specs(d, tm, blocks_per_mod):
    mod_spec = pl.BlockSpec((None, 1, d), lambda i: (i // blocks_per_mod, 0, 0))
    in_specs = [pl.BlockSpec((1, d), lambda i: (0, 0)), mod_spec, mod_spec,
                pl.BlockSpec((N_EXPERTS, d), lambda i: (0, 0))]
    out_specs = [pl.BlockSpec((tm, d), lambda i: (i, 0)),
                 pl.BlockSpec((tm, d // LANES, LANES), lambda i: (i, 0, 0)),
                 pl.BlockSpec((tm // LANES, N_EXPERTS, LANES), lambda i: (i, 0, 0))]
    return in_specs, out_specs


def _epilogue_out_shape(n, d):
    return [jax.ShapeDtypeStruct((n, d), F32),
            jax.ShapeDtypeStruct((n, d // LANES, LANES), F32),
            jax.ShapeDtypeStruct((n // LANES, N_EXPERTS, LANES), F32)]


def _proj_mixer(o, w_bf16, x2d, gate1, g2, sh2, sc2, wr_t, tokens_per_mod):
    n, d = x2d.shape
    tm = ROW_TILE
    kdim = o.shape[1]
    bpm = tokens_per_mod // tm
    ep_in, ep_out = _epilogue_specs(d, tm, bpm)
    return pl.pallas_call(
        _proj_kernel,
        grid=(n // tm,),
        in_specs=[pl.BlockSpec((tm, kdim), lambda i: (i, 0)),
                  pl.BlockSpec((kdim, d), lambda i: (0, 0)),
                  pl.BlockSpec((tm, d), lambda i: (i, 0)),
                  pl.BlockSpec((None, 1, d), lambda i: (i // bpm, 0, 0))] + ep_in,
        out_specs=ep_out,
        out_shape=_epilogue_out_shape(n, d),
        compiler_params=_cparams(("parallel",)),
        name="proj_mixer",
    )(o, w_bf16, x2d, gate1, g2.reshape(1, d), sh2, sc2, wr_t)


def _pool_kernel(x_ref, xprev_ref, xnext_ref, g1_ref, sh1_ref, sc1_ref, pw_ref, pb_ref, ps_ref,
                 gate_ref, g2_ref, sh2_ref, sc2_ref, wr_ref, xo_ref, h2_ref, aff_ref, *, seq_len):
    tm, d = x_ref.shape
    halo = POOL_HALO
    i = pl.program_id(0)
    pos0 = (i * tm) % seq_len
    g1, sh1, sc1 = g1_ref[...], sh1_ref[...], sc1_ref[...]
    x = x_ref[...]
    h = _adaln_rows(x, g1, sh1, sc1)
    hprev = _adaln_rows(xprev_ref[...], g1, sh1, sc1)
    hnext = _adaln_rows(xnext_ref[...], g1, sh1, sc1)
    hprev = jnp.where(pos0 > 0, hprev, 0.0)
    hnext = jnp.where(pos0 + tm < seq_len, hnext, 0.0)
    hext = jnp.concatenate([hprev, h, hnext], axis=0)
    rows = tm + 2 * halo
    t = pos0 + lax.broadcasted_iota(jnp.int32, (tm, 1), 0)
    outs = []
    for gi, win in enumerate(POOL_WINDOWS):
        a = hext[:, gi * POOL_GROUP:(gi + 1) * POOL_GROUP]
        p = pltpu.roll(a, 1, axis=0) + a
        step = 1
        while 2 * step < win:
            p = pltpu.roll(p, step, axis=0) + pltpu.roll(p, rows - step, axis=0)
            step *= 2
        half = win // 2
        cnt = (jnp.minimum(t + half, seq_len) - jnp.maximum(t - half, 0)).astype(F32)
        pooled = p[halo:halo + tm, :] / cnt - a[halo:halo + tm, :]
        y = jnp.dot(pooled.astype(BF16), pw_ref[gi].astype(BF16), preferred_element_type=F32)
        outs.append(y + pb_ref[gi:gi + 1, :])
    mix = jnp.concatenate(outs, axis=1) * ps_ref[...]
    x_new = x + gate_ref[...] * mix
    _epilogue(x_new, g2_ref[...], sh2_ref[...], sc2_ref[...], wr_ref[...], xo_ref, h2_ref, aff_ref)


def _pool_mixer(x2d, g1, sh1, sc1, pool_w, pool_b, pool_scale, gate1, g2, sh2, sc2, wr_t,
                tokens_per_mod, seq_len):
    n, d = x2d.shape
    tm = min(ROW_TILE, seq_len)
    bpm = tokens_per_mod // tm
    hb = tm // POOL_HALO
    n_hblocks = n // POOL_HALO
    mod_spec = pl.BlockSpec((None, 1, d), lambda i: (i // bpm, 0, 0))
    ep_in, ep_out = _epilogue_specs(d, tm, bpm)
    return pl.pallas_call(
        functools.partial(_pool_kernel, seq_len=seq_len),
        grid=(n // tm,),
        in_specs=[pl.BlockSpec((tm, d), lambda i: (i, 0)),
                  pl.BlockSpec((POOL_HALO, d), lambda i: (jnp.maximum(i * hb - 1, 0), 0)),
                  pl.BlockSpec((POOL_HALO, d),
                               lambda i: (jnp.minimum((i + 1) * hb, n_hblocks - 1), 0)),
                  pl.BlockSpec((1, d), lambda i: (0, 0)), mod_spec, mod_spec,
                  pl.BlockSpec((N_POOL_GROUPS, POOL_GROUP, POOL_GROUP), lambda i: (0, 0, 0)),
                  pl.BlockSpec((N_POOL_GROUPS, POOL_GROUP), lambda i: (0, 0)),
                  pl.BlockSpec((1, d), lambda i: (0, 0)),
                  mod_spec] + ep_in,
        out_specs=ep_out,
        out_shape=_epilogue_out_shape(n, d),
        compiler_params=_cparams(("parallel",)),
        name="pool_mixer",
    )(x2d, x2d, x2d, g1.reshape(1, d), sh1, sc1, pool_w, pool_b, pool_scale.reshape(1, d),
      gate1, g2.reshape(1, d), sh2, sc2, wr_t)


def _ffn_gather_copy(h2p_hbm, xbuf, sem, slot, src_row, dst_row, nrows):
    return pltpu.make_async_copy(h2p_hbm.at[pl.ds(src_row, nrows)],
                                 xbuf.at[slot, pl.ds(dst_row, nrows)], sem.at[slot])


def _ffn_kernel(idx_ref, h2a_hbm, h2b_hbm, wg_ref, wu_ref, wd_ref, g_ref, ye_ref, xbuf, sem):
    n_e, n_s = pl.num_programs(0), pl.num_programs(1)
    ei, si = pl.program_id(0), pl.program_id(1)
    cap, pieces = xbuf.shape[1], xbuf.shape[2]
    rows = min(FFN_ROWS, cap)
    step = ei * n_s + si
    slot = step % 2

    def issue(e_t, s_t, slot_t):
        base = (s_t * n_e + e_t) * cap
        for set_id, src in enumerate((h2a_hbm, h2b_hbm)):
            @pl.when(s_t == set_id)
            def _(src=src):
                def body(kb, carry):
                    k0 = kb * GATHER_UNROLL
                    for u in range(GATHER_UNROLL):
                        pltpu.async_copy(src.at[pl.ds(idx_ref[base + k0 + u], 1)],
                                         xbuf.at[slot_t, pl.ds(k0 + u, 1)], sem.at[slot_t],
                                         priority=u % 2)
                    return carry

                lax.fori_loop(0, cap // GATHER_UNROLL, body, 0)

    @pl.when(step == 0)
    def _():
        issue(ei, si, slot)

    @pl.when(step + 1 < n_e * n_s)
    def _():
        nxt = step + 1
        issue(nxt // n_s, nxt % n_s, 1 - slot)

    _ffn_gather_copy(h2a_hbm, xbuf, sem, slot, 0, 0, cap).wait()

    def body(i, carry):
        r0 = pl.multiple_of(i * rows, rows)
        x = jnp.concatenate([xbuf[slot, pl.ds(r0, rows), j, :].astype(BF16)
                             for j in range(pieces)], axis=1)
        a = jnp.dot(x, wg_ref[...], preferred_element_type=F32)
        b = jnp.dot(x, wu_ref[...], preferred_element_type=F32)
        hid = (a * (1.0 / (1.0 + jnp.exp(-a))) * b).astype(BF16)
        y = jnp.dot(hid, wd_ref[...], preferred_element_type=F32)
        gate = g_ref[pl.ds(r0, rows), :]
        for j in range(y.shape[1] // LANES):
            ye_ref[pl.ds(r0, rows), j * LANES:(j + 1) * LANES] = (
                y[:, j * LANES:(j + 1) * LANES] * gate).astype(ye_ref.dtype)
        return carry

    lax.fori_loop(0, cap // rows, body, 0)


def _expert_ffn(idx_flat, h2a, h2b, wg, wu, wd, gates, layer):
    s, e, c, _ = gates.shape
    assert s == 2
    d, f = wg.shape[2], wg.shape[3]
    return pl.pallas_call(
        _ffn_kernel,
        grid_spec=pltpu.PrefetchScalarGridSpec(
            num_scalar_prefetch=1,
            grid=(e, s),
            in_specs=[pl.BlockSpec(memory_space=pl.ANY),
                      pl.BlockSpec(memory_space=pl.ANY),
                      pl.BlockSpec((None, None, d, f), lambda ei, si, idx: (layer, ei, 0, 0)),
                      pl.BlockSpec((None, None, d, f), lambda ei, si, idx: (layer, ei, 0, 0)),
                      pl.BlockSpec((None, None, f, d), lambda ei, si, idx: (layer, ei, 0, 0)),
                      pl.BlockSpec((None, None, c, LANES), lambda ei, si, idx: (si, ei, 0, 0))],
            out_specs=pl.BlockSpec((None, None, c, d), lambda ei, si, idx: (si, ei, 0, 0)),
            scratch_shapes=[pltpu.VMEM((2, c, d // LANES, LANES), F32),
                            pltpu.SemaphoreType.DMA((2,))]),
        out_shape=jax.ShapeDtypeStruct((s, e, c, d), BF16),
        compiler_params=pltpu.CompilerParams(dimension_semantics=("arbitrary", "arbitrary"),
                                             vmem_limit_bytes=VMEM_LIMIT,
                                             disable_bounds_checks=True),
        name="expert_ffn",
    )(idx_flat, h2a, h2b, wg, wu, wd, gates)


def _combine_kernel(clo_ref, ncnt_ref, ye_hbm, tok_hbm, x_ref, gate_ref, fg_ref, o_ref,
                    zbuf, tokbuf, acc_ref, sem, *, final):
    n_e = ye_hbm.shape[0]
    tm, d = x_ref.shape
    ch, grp = COMBINE_CH, COMBINE_GROUP
    i = pl.program_id(0)
    nt = pl.num_programs(0)
    slot = i % 2

    def chunk_copies(e, src_row, slot_t, dst_row):
        return (pltpu.make_async_copy(ye_hbm.at[e, pl.ds(src_row, ch), :],
                                      zbuf.at[slot_t, pl.ds(dst_row, ch), :], sem.at[slot_t]),
                pltpu.make_async_copy(tok_hbm.at[e, pl.ds(src_row, ch), :],
                                      tokbuf.at[slot_t, pl.ds(dst_row, ch), :], sem.at[slot_t]))

    def issue(tile, slot_t):
        pos = 0
        for e in range(n_e):
            lo = clo_ref[tile * n_e + e]

            def body(j, p, e=e, lo=lo):
                src = pl.multiple_of((lo + j) * ch, ch)
                dst = pl.multiple_of(p * ch, ch)
                for cp in chunk_copies(e, src, slot_t, dst):
                    cp.start()
                return p + 1

            pos = lax.fori_loop(0, ncnt_ref[tile * n_e + e], body, pos)

    def total_chunks(tile):
        tot = 0
        for e in range(n_e):
            tot = tot + ncnt_ref[tile * n_e + e]
        return tot

    @pl.when(i == 0)
    def _():
        zbuf[...] = jnp.zeros(zbuf.shape, zbuf.dtype)
        tokbuf[...] = jnp.full(tokbuf.shape, -1, jnp.int32)
        issue(i, slot)

    @pl.when(i + 1 < nt)
    def _():
        issue(i + 1, 1 - slot)

    total = total_chunks(i)

    def wait_body(j, carry):
        for cp in chunk_copies(0, 0, slot, 0):
            cp.wait()
        return carry

    lax.fori_loop(0, total, wait_body, 0)

    acc_ref[...] = jnp.zeros(acc_ref.shape, F32)
    tok_of_lane = i * tm + lax.broadcasted_iota(jnp.int32, (grp, tm), 1)
    row_in_group = lax.broadcasted_iota(jnp.int32, (grp, 1), 0)

    def group_body(gidx, carry):
        r0 = pl.multiple_of(gidx * grp, grp)
        tok = jnp.where(r0 + row_in_group < total * ch, tokbuf[slot, pl.ds(r0, grp), :], -1)
        tok = jnp.tile(tok, (1, tm // LANES))
        onehot_t = (tok == tok_of_lane).astype(BF16)
        acc_ref[...] += lax.dot_general(onehot_t, zbuf[slot, pl.ds(r0, grp), :],
                                        (((0,), (0,)), ((), ())), preferred_element_type=F32)
        return carry

    lax.fori_loop(0, (total * ch + grp - 1) // grp, group_body, 0)

    x = x_ref[...] + gate_ref[...] * acc_ref[...]
    if final:
        ms = jnp.mean(x * x, axis=-1, keepdims=True)
        x = x * lax.rsqrt(ms + EPS) * fg_ref[...]
    o_ref[...] = x


def _combine(ye, tok_col, toff, x2d, gate2, final_g, tokens_per_mod, final):
    n, d = x2d.shape
    e, c, _ = ye.shape
    tm, ch = COMBINE_TM, COMBINE_CH
    n_tiles = n // tm
    bpm = tokens_per_mod // tm
    start = toff[::tm // LANES, :, 0]
    first = jnp.concatenate([start, jnp.full((1, e), c, jnp.int32)], axis=0)
    clo = first[:-1] // ch
    chi = (first[1:] + ch - 1) // ch
    ncnt = jnp.maximum(chi - clo, 0)
    max_rows = e * (tm // ch + 2) * ch
    max_rows = ((max_rows + COMBINE_GROUP - 1) // COMBINE_GROUP) * COMBINE_GROUP
    return pl.pallas_call(
        functools.partial(_combine_kernel, final=final),
        grid_spec=pltpu.PrefetchScalarGridSpec(
            num_scalar_prefetch=2,
            grid=(n_tiles,),
            in_specs=[pl.BlockSpec(memory_space=pl.ANY),
                      pl.BlockSpec(memory_space=pl.ANY),
                      pl.BlockSpec((tm, d), lambda i, a, b: (i, 0)),
                      pl.BlockSpec((None, 1, d), lambda i, a, b: (i // bpm, 0, 0)),
                      pl.BlockSpec((1, d), lambda i, a, b: (0, 0))],
            out_specs=pl.BlockSpec((tm, d), lambda i, a, b: (i, 0)),
            scratch_shapes=[pltpu.VMEM((2, max_rows, d), BF16),
                            pltpu.VMEM((2, max_rows, LANES), jnp.int32),
                            pltpu.VMEM((tm, d), F32),
                            pltpu.SemaphoreType.DMA((2,))]),
        out_shape=jax.ShapeDtypeStruct((n, d), F32),
        compiler_params=_cparams(("arbitrary",)),
        name="combine_final" if final else "combine",
    )(clo.reshape(-1), ncnt.reshape(-1), ye, tok_col, x2d, gate2, final_g.reshape(1, d))


def _rope_tables(n_tokens):
    t = jnp.arange(n_tokens)
    row = (t // GRID_W).astype(F32)
    col = (t % GRID_W).astype(F32)
    half = HEAD_DIM // 2
    inv = ROPE_BASE ** (-jnp.arange(0, half, 2, dtype=F32) / half)
    ar, ac = row[:, None] * inv, col[:, None] * inv
    c64 = jnp.concatenate([jnp.cos(ar), jnp.cos(ar), jnp.cos(ac), jnp.cos(ac)], axis=-1)
    s64 = jnp.concatenate([-jnp.sin(ar), jnp.sin(ar), -jnp.sin(ac), jnp.sin(ac)], axis=-1)
    return jnp.tile(c64, (1, 2)), jnp.tile(s64, (1, 2))


def _tile_prefix(mask3, il_ref, to_ref, tot_ref):
    nt, ne, ln = mask3.shape
    m2 = mask3.astype(F32).astype(BF16).reshape(nt * ne, ln)
    r = lax.broadcasted_iota(jnp.int32, (ln, ln), 0)
    c = lax.broadcasted_iota(jnp.int32, (ln, ln), 1)
    upper = (r <= c).astype(F32).astype(BF16)
    ones = jnp.ones((ln, ln), BF16)
    il_ref[...] = jnp.dot(m2, upper, preferred_element_type=F32).reshape(nt, ne, ln)
    tot_ref[...] = jnp.dot(m2, ones, preferred_element_type=F32).reshape(nt, ne, ln)
    run = jnp.zeros((ne, ln), F32)
    for tt in range(nt):
        to_ref[tt] = run
        run = run + tot_ref[tt]


def _split3_bf16(v):
    p1 = v.astype(BF16)
    r1 = v - p1.astype(F32)
    p2 = r1.astype(BF16)
    return p1, p2, (r1 - p2.astype(F32)).astype(BF16)


def _route_kernel(aff_ref, idx_ref, gcol_ref, tokcol_ref, toff_ref, il_ref, to_ref, tot_ref, *, cap):
    nt, ne, ln = aff_ref.shape
    a3 = aff_ref[...]

    def count(mask3):
        per_lane = jnp.sum(mask3.astype(jnp.int32), axis=0)
        return jnp.sum(per_lane, axis=1, keepdims=True)

    def as_float(thr_bits):
        return pltpu.bitcast(jnp.broadcast_to(thr_bits, (ne, ln)), F32)[:, 0:1]

    def search(i, thr_bits):
        cand = thr_bits | lax.shift_left(jnp.int32(1), 30 - i)
        return jnp.where(count(a3 >= as_float(cand)[None]) >= cap, cand, thr_bits)

    thr = as_float(lax.fori_loop(0, 31, search, jnp.zeros((ne, 1), jnp.int32)))
    gt = a3 > thr[None]
    eq = a3 == thr[None]
    need = (cap - count(gt)).astype(F32)
    _tile_prefix(eq, il_ref, to_ref, tot_ref)
    eq_before = to_ref[...] + il_ref[...] - eq.astype(F32)
    sel = gt | (eq & (eq_before < need[None]))
    _tile_prefix(sel, il_ref, to_ref, tot_ref)
    toff_ref[...] = to_ref[...].astype(jnp.int32)

    reps = cap // ln
    ones = jnp.ones((ln, ln), BF16)

    def to_column(vals, bf16_exact=False):
        out = None
        for piece in ((vals.astype(BF16),) if bf16_exact else _split3_bf16(vals)):
            t = lax.dot_general(piece, ones, (((0,), (0,)), ((), ())), preferred_element_type=F32)
            out = t if out is None else out + t
        return out

    slot = lax.broadcasted_iota(jnp.int32, (nt, cap), 1).astype(F32)
    tile_id = lax.broadcasted_iota(jnp.int32, (nt, cap), 0).astype(F32)
    in_tile = lax.broadcasted_iota(jnp.int32, (ln, cap), 0).astype(F32)
    for e in range(ne):
        il_e = il_ref[:, e, :]
        to_e = jnp.tile(to_ref[:, e, :], (1, reps))
        tot_e = jnp.tile(tot_ref[:, e, :], (1, reps))
        owns = (to_e <= slot) & (slot < to_e + tot_e)
        rank = jnp.sum(jnp.where(owns, slot - to_e, 0.0), axis=0, keepdims=True)
        tile_of = jnp.sum(jnp.where(owns, tile_id, 0.0), axis=0, keepdims=True)
        owns_b = owns.astype(F32).astype(BF16)
        contract0 = (((0,), (0,)), ((), ()))
        il_g = lax.dot_general(il_e.astype(BF16), owns_b, contract0,
                               preferred_element_type=F32)
        pos = jnp.sum((il_g <= rank).astype(F32), axis=0, keepdims=True)
        idx_ref[e:e + 1, :] = (tile_of * ln + pos).astype(jnp.int32)
        p1, p2, p3 = _split3_bf16(aff_ref[:, e, :])
        a_g = (lax.dot_general(p1, owns_b, contract0, preferred_element_type=F32)
               + lax.dot_general(p2, owns_b, contract0, preferred_element_type=F32)
               + lax.dot_general(p3, owns_b, contract0, preferred_element_type=F32))
        hit = in_tile == pos
        gcol_ref[e] = to_column(jnp.where(hit, a_g, 0.0))
        tok_hi = to_column(jnp.where(hit, tile_of, 0.0), bf16_exact=True)
        tok_lo = to_column(jnp.where(hit, pos, 0.0), bf16_exact=True)
        tokcol_ref[e] = (tok_hi * ln + tok_lo).astype(jnp.int32)


def _route(aff_all, cap):
    s, nt, ne, ln = aff_all.shape
    assert nt <= 256 and cap % ln == 0
    return pl.pallas_call(
        functools.partial(_route_kernel, cap=cap),
        grid=(s,),
        in_specs=[pl.BlockSpec((None, nt, ne, ln), lambda i: (i, 0, 0, 0))],
        out_specs=[pl.BlockSpec((None, ne, cap), lambda i: (i, 0, 0)),
                   pl.BlockSpec((None, ne, cap, ln), lambda i: (i, 0, 0, 0)),
                   pl.BlockSpec((None, ne, cap, ln), lambda i: (i, 0, 0, 0)),
                   pl.BlockSpec((None, nt, ne, ln), lambda i: (i, 0, 0, 0))],
        out_shape=[jax.ShapeDtypeStruct((s, ne, cap), jnp.int32),
                   jax.ShapeDtypeStruct((s, ne, cap, ln), F32),
                   jax.ShapeDtypeStruct((s, ne, cap, ln), jnp.int32),
                   jax.ShapeDtypeStruct((s, nt, ne, ln), jnp.int32)],
        scratch_shapes=[pltpu.VMEM((nt, ne, ln), F32)] * 3,
        compiler_params=_cparams(("parallel",)),
        name="route",
    )(aff_all)


def _route_and_ffn(h2_list, aff_list, w_gate, w_up, w_down, layer):
    n = h2_list[0].shape[0]
    assert all(h.shape[0] == n for h in h2_list)
    cap = EC_CAPACITY * n // N_EXPERTS
    idx_all, gcol, tokcol, toff = _route(jnp.stack(aff_list), cap)
    ye = _expert_ffn(idx_all.reshape(-1), h2_list[0], h2_list[1], w_gate, w_up, w_down, gcol, layer)
    return ye, tokcol, toff


def kernel(x_prompt, x_sample, cache_k, cache_v, c, c_ctx, w_mod, b_mod, norm1_g, norm2_g,
           attn_wqkv, attn_wo, lambda_q1, lambda_k1, lambda_q2, lambda_k2, subln_g,
           pool_w, pool_b, pool_scale, router_w, expert_w_gate, expert_w_up, expert_w_down,
           final_g):
    bp, n_ctx, d = x_prompt.shape
    bs, n_lat, _ = x_sample.shape
    past = cache_k.shape[2]
    n_p, n_s = bp * n_ctx, bs * n_lat

    rows = 8 * ((1 + bs + 7) // 8)
    cvecs = jnp.zeros((rows, d), F32).at[0].set(c_ctx).at[1:1 + bs].set(c)
    mod = _modulation_all(cvecs, w_mod, b_mod)
    rope_tabs = _rope_tables(n_lat)
    cache_k4 = cache_k.reshape(bs, cache_k.shape[1], past, ATTN_WIDTH)
    cache_v4 = cache_v.reshape(bs, cache_v.shape[1], past, ATTN_WIDTH)
    w_gate_b = expert_w_gate.astype(BF16)
    w_up_b = expert_w_up.astype(BF16)
    w_down_b = expert_w_down.astype(BF16)

    xp = x_prompt.reshape(n_p, d)
    xs = x_sample.reshape(n_s, d)
    new_k, new_v = [], []
    for l in range(DEPTH):
        mp_ = mod[l, 0:1].reshape(1, 6, 1, d)
        ms_ = mod[l, 1:1 + bs].reshape(bs, 6, 1, d)
        sp1, cp1, gp1, sp2, cp2, gp2 = [mp_[:, i] for i in range(6)]
        ss1, cs1, gs1, ss2, cs2, gs2 = [ms_[:, i] for i in range(6)]
        wr_t = router_w[l].T.astype(BF16)
        if l % 2 == 0:
            a = l // 2
            lam_init = 0.8 - 0.6 * math.exp(-0.3 * l)
            lamv = jnp.stack([lambda_q1[a], lambda_k1[a], lambda_q2[a], lambda_k2[a]])
            wqkv = attn_wqkv[a].astype(BF16)
            wo = attn_wo[a].astype(BF16)
            qp, kp, vp = _qkv_project(xp, norm1_g[l], sp1, cp1, wqkv, None, n_p, F32)
            new_k.append(kp.reshape(bp, n_ctx, N_HEADS, 2 * HEAD_DIM))
            new_v.append(vp.reshape(bp, n_ctx, N_HEADS, V_DIM))
            op = _diff_attention(qp, kp, vp, lamv, subln_g[a], lam_init, bp)
            xp, h2p, affp = _proj_mixer(op, wo, xp, gp1, norm2_g[l], sp2, cp2, wr_t, n_p)
            qs, ks_, vs = _qkv_project(xs, norm1_g[l], ss1, cs1, wqkv, rope_tabs, n_lat, BF16)
            os_ = _diff_attention(qs, ks_, vs, lamv, subln_g[a], lam_init, bs,
                                  cache=(cache_k4, cache_v4, a))
            xs, h2s, affs = _proj_mixer(os_, wo, xs, gs1, norm2_g[l], ss2, cs2, wr_t, n_lat)
        else:
            p = l // 2
            xp, h2p, affp = _pool_mixer(xp, norm1_g[l], sp1, cp1, pool_w[p], pool_b[p],
                                        pool_scale[p], gp1, norm2_g[l], sp2, cp2, wr_t,
                                        n_p, n_ctx)
            xs, h2s, affs = _pool_mixer(xs, norm1_g[l], ss1, cs1, pool_w[p], pool_b[p],
                                        pool_scale[p], gs1, norm2_g[l], ss2, cs2, wr_t,
                                        n_lat, n_lat)
        ye, tokcol, toff = _route_and_ffn([h2p, h2s], [affp, affs], w_gate_b, w_up_b, w_down_b, l)
        last = l == DEPTH - 1
        xp = _combine(ye[0], tokcol[0], toff[0], xp, gp2, final_g, n_p, last)
        xs = _combine(ye[1], tokcol[1], toff[1], xs, gs2, final_g, n_lat, last)
    return (xp.reshape(bp, n_ctx, d), xs.reshape(bs, n_lat, d),
            jnp.stack(new_k, axis=1), jnp.stack(new_v, axis=1))
```

```python
import math
import functools
import jax
import jax.numpy as jnp
from jax import lax
from jax.experimental import pallas as pl
from jax.experimental.pallas import tpu as pltpu

D_MODEL = 1024
DEPTH = 4
GRID_W = 64
N_HEADS = 8
HEAD_DIM = 64
V_DIM = 2 * HEAD_DIM
ATTN_WIDTH = N_HEADS * V_DIM
ROPE_BASE = 10000.0
N_POOL_GROUPS = 4
POOL_GROUP = D_MODEL // N_POOL_GROUPS
POOL_WINDOWS = (2, 4, 8, 16)
POOL_HALO = 8
N_EXPERTS = 16
EC_CAPACITY = 2
EPS = 1e-6

F32 = jnp.float32
BF16 = jnp.bfloat16
LANES = 128
LOG2E = 1.4426950408889634
VMEM_LIMIT = 56 * 1024 * 1024
ROW_TILE = 512
ATTN_TQ = 256
ATTN_TK = 1024
ATTN_HEADS_PER_STEP = 2
FFN_ROWS = 512
SUBLANES = 8
COMBINE_TM = 256
COMBINE_CH = 32
COMBINE_GROUP = 256


def _cparams(sem):
    return pltpu.CompilerParams(dimension_semantics=sem, vmem_limit_bytes=VMEM_LIMIT)


def _mod_kernel(c_ref, w_ref, b_ref, o_ref):
    c = c_ref[...]
    s = (c * (1.0 / (1.0 + jnp.exp(-c)))).astype(BF16)
    o_ref[...] = jnp.dot(s, w_ref[...].astype(BF16), preferred_element_type=F32) + b_ref[...]


def _modulation_all(cvecs, w_mod, b_mod):
    r, d = cvecs.shape
    depth, _, n6 = w_mod.shape
    tn = 1024
    return pl.pallas_call(
        _mod_kernel,
        grid=(depth, n6 // tn),
        in_specs=[pl.BlockSpec((r, d), lambda l, j: (0, 0)),
                  pl.BlockSpec((None, d, tn), lambda l, j: (l, 0, j)),
                  pl.BlockSpec((None, 1, tn), lambda l, j: (l, 0, j))],
        out_specs=pl.BlockSpec((None, r, tn), lambda l, j: (l, 0, j)),
        out_shape=jax.ShapeDtypeStruct((depth, r, n6), F32),
        compiler_params=_cparams(("parallel", "parallel")),
        name="modulation",
    )(cvecs, w_mod, b_mod.reshape(depth, 1, n6))


def _adaln_rows(x, g, shift, scale):
    ms = jnp.mean(x * x, axis=-1, keepdims=True)
    return (x * lax.rsqrt(ms + EPS)) * g * (1.0 + scale) + shift


def _swap16(x):
    lane = lax.broadcasted_iota(jnp.int32, x.shape, 1)
    up = pltpu.roll(x, x.shape[1] - 16, axis=1)
    dn = pltpu.roll(x, 16, axis=1)
    return jnp.where((lane % 32) < 16, up, dn)


def _qkv_kernel(*refs, rope):
    if rope:
        x_ref, g_ref, sh_ref, sc_ref, w_ref, cos_ref, sin_ref, q_ref, k_ref, v_ref = refs
    else:
        x_ref, g_ref, sh_ref, sc_ref, w_ref, q_ref, k_ref, v_ref = refs
    h = _adaln_rows(x_ref[...], g_ref[...], sh_ref[...], sc_ref[...]).astype(BF16)
    width = q_ref.shape[1]
    qscale = (HEAD_DIM ** -0.5) * LOG2E
    for part, o_ref in enumerate((q_ref, k_ref, v_ref)):
        r = jnp.dot(h, w_ref[:, part * width:(part + 1) * width], preferred_element_type=F32)
        if part < 2 and rope:
            c = cos_ref[...]
            s = sin_ref[...]
            for hd in range(width // LANES):
                blk = r[:, hd * LANES:(hd + 1) * LANES]
                blk = blk * c + _swap16(blk) * s
                if part == 0:
                    blk = blk * qscale
                o_ref[:, hd * LANES:(hd + 1) * LANES] = blk.astype(o_ref.dtype)
        else:
            if part == 0:
                r = r * qscale
            o_ref[...] = r.astype(o_ref.dtype)


def _qkv_project(x2d, g, shift, scale, w_bf16, rope_tabs, tokens_per_mod, kv_dtype):
    n, d = x2d.shape
    tm = ROW_TILE
    width = w_bf16.shape[1] // 3
    rope = rope_tabs is not None
    blocks_per_mod = tokens_per_mod // tm
    in_specs = [pl.BlockSpec((tm, d), lambda i: (i, 0)),
                pl.BlockSpec((1, d), lambda i: (0, 0)),
                pl.BlockSpec((None, 1, d), lambda i: (i // blocks_per_mod, 0, 0)),
                pl.BlockSpec((None, 1, d), lambda i: (i // blocks_per_mod, 0, 0)),
                pl.BlockSpec((d, 3 * width), lambda i: (0, 0))]
    args = [x2d, g.reshape(1, d), shift, scale, w_bf16]
    if rope:
        cos_t, sin_t = rope_tabs
        seq_blocks = cos_t.shape[0] // tm
        in_specs += [pl.BlockSpec((tm, LANES), lambda i: (i % seq_blocks, 0)),
                     pl.BlockSpec((tm, LANES), lambda i: (i % seq_blocks, 0))]
        args += [cos_t, sin_t]
    out_spec = pl.BlockSpec((tm, width), lambda i: (i, 0))
    return pl.pallas_call(
        functools.partial(_qkv_kernel, rope=rope),
        grid=(n // tm,),
        in_specs=in_specs,
        out_specs=[out_spec, out_spec, out_spec],
        out_shape=[jax.ShapeDtypeStruct((n, width), BF16),
                   jax.ShapeDtypeStruct((n, width), kv_dtype),
                   jax.ShapeDtypeStruct((n, width), kv_dtype)],
        compiler_params=_cparams(("parallel",)),
        name="adaln_qkv_rope" if rope else "adaln_qkv",
    )(*args)


def _attn_kernel(*refs, tk, lam_init, hp, cached):
    if cached:
        (lamv_ref, q_ref, k_ref, v_ref, kc_ref, vc_ref, g_ref, o_ref,
         qs_ref, m_ref, l_ref, acc_ref) = refs
    else:
        lamv_ref, q_ref, k_ref, v_ref, g_ref, o_ref, qs_ref, m_ref, l_ref, acc_ref = refs
    tq = q_ref.shape[0]
    nk = k_ref.shape[0]
    for h in range(hp):
        q = q_ref[:, h * LANES:(h + 1) * LANES]
        lane = lax.broadcasted_iota(jnp.int32, q.shape, 1)
        zero = jnp.zeros_like(q)
        qs_ref[h, 0:tq, :] = jnp.where(lane < HEAD_DIM, q, zero)
        qs_ref[h, tq:2 * tq, :] = jnp.where(lane >= HEAD_DIM, q, zero)
    m_ref[...] = jnp.full(m_ref.shape, -jnp.inf, F32)
    l_ref[...] = jnp.zeros(l_ref.shape, F32)
    acc_ref[...] = jnp.zeros(acc_ref.shape, F32)

    def online_softmax_step(h, kc, vc):
        s = lax.dot_general(qs_ref[h], kc, (((1,), (1,)), ((), ())),
                            preferred_element_type=F32)
        m_old = m_ref[h]
        m_new = jnp.maximum(m_old, jnp.max(s, axis=1)[:, None])
        alpha = jnp.exp2(m_old - m_new)
        p = jnp.exp2(s - jnp.tile(m_new, (1, s.shape[1] // LANES)))
        l_ref[h] = alpha * l_ref[h] + jnp.sum(p, axis=1)[:, None]
        acc_ref[h] = alpha * acc_ref[h] + jnp.dot(p.astype(BF16), vc, preferred_element_type=F32)
        m_ref[h] = m_new

    def body(j, carry):
        off = pl.multiple_of(j * tk, tk)
        for h in range(hp):
            online_softmax_step(h, k_ref[pl.ds(off, tk), h * LANES:(h + 1) * LANES].astype(BF16),
                                v_ref[pl.ds(off, tk), h * LANES:(h + 1) * LANES].astype(BF16))
        return carry

    lax.fori_loop(0, nk // tk, body, 0)
    if cached:
        for h in range(hp):
            online_softmax_step(h, kc_ref[:, h * LANES:(h + 1) * LANES].astype(BF16),
                                vc_ref[:, h * LANES:(h + 1) * LANES].astype(BF16))

    lv = lamv_ref[...]
    lam = (jnp.exp(jnp.sum(lv[0:1, :] * lv[1:2, :], axis=-1, keepdims=True))
           - jnp.exp(jnp.sum(lv[2:3, :] * lv[3:4, :], axis=-1, keepdims=True)) + lam_init)
    for h in range(hp):
        o = acc_ref[h] / l_ref[h]
        d = o[0:tq, :] - lam * o[tq:2 * tq, :]
        ms = jnp.mean(d * d, axis=-1, keepdims=True)
        o_ref[:, h * LANES:(h + 1) * LANES] = (
            (d * lax.rsqrt(ms + EPS)) * g_ref[...] * (1.0 - lam_init)).astype(o_ref.dtype)


def _diff_attention(q, k, v, lamv, subln_g, lam_init, batch, cache=None):
    nq = q.shape[0] // batch
    nk = k.shape[0] // batch
    tq = min(ATTN_TQ, nq)
    tk = min(ATTN_TK, nk)
    assert nq % tq == 0 and nk % tk == 0 and tk % LANES == 0
    hp = ATTN_HEADS_PER_STEP
    w = hp * LANES
    q3 = q.reshape(batch, nq, ATTN_WIDTH)
    k3 = k.reshape(batch, nk, ATTN_WIDTH)
    v3 = v.reshape(batch, nk, ATTN_WIDTH)
    kv_spec = pl.BlockSpec((None, nk, w), lambda b, h, i: (b, 0, h))
    in_specs = [pl.BlockSpec((4, HEAD_DIM), lambda b, h, i: (0, 0)),
                pl.BlockSpec((None, tq, w), lambda b, h, i: (b, i, h)), kv_spec, kv_spec]
    args = [lamv, q3, k3, v3]
    if cache is not None:
        ck, cv, layer = cache
        past = ck.shape[2]
        assert past % LANES == 0
        c_spec = pl.BlockSpec((None, None, past, w), lambda b, h, i: (b, layer, 0, h))
        in_specs += [c_spec, c_spec]
        args += [ck, cv]
    in_specs.append(pl.BlockSpec((1, V_DIM), lambda b, h, i: (0, 0)))
    args.append(subln_g.reshape(1, V_DIM))
    out = pl.pallas_call(
        functools.partial(_attn_kernel, tk=tk, lam_init=lam_init, hp=hp, cached=cache is not None),
        grid=(batch, N_HEADS // hp, nq // tq),
        in_specs=in_specs,
        out_specs=pl.BlockSpec((None, tq, w), lambda b, h, i: (b, i, h)),
        out_shape=jax.ShapeDtypeStruct((batch, nq, ATTN_WIDTH), BF16),
        scratch_shapes=[pltpu.VMEM((hp, 2 * tq, V_DIM), BF16),
                        pltpu.VMEM((hp, 2 * tq, LANES), F32),
                        pltpu.VMEM((hp, 2 * tq, LANES), F32),
                        pltpu.VMEM((hp, 2 * tq, V_DIM), F32)],
        compiler_params=_cparams(("parallel", "parallel", "parallel")),
        name="diff_attention",
    )(*args)
    return out.reshape(batch * nq, ATTN_WIDTH)


def _epilogue(x_new, g2, sh2, sc2, wr_t, xo_ref, h2_ref, aff_ref):
    xo_ref[...] = x_new
    h2 = _adaln_rows(x_new, g2, sh2, sc2)
    h2b = h2.astype(BF16)
    h2_ref[...] = h2
    logits = lax.dot_general(wr_t, h2b, (((1,), (1,)), ((), ())),
                             preferred_element_type=F32)
    mx = jnp.max(logits, axis=0, keepdims=True)
    ex = jnp.exp(logits - mx)
    aff = ex / jnp.sum(ex, axis=0, keepdims=True)
    for r in range(aff_ref.shape[0]):
        aff_ref[r] = aff[:, r * LANES:(r + 1) * LANES]


def _proj_kernel(o_ref, w_ref, x_ref, gate_ref, g2_ref, sh2_ref, sc2_ref, wr_ref,
                 xo_ref, h2_ref, aff_ref):
    y = jnp.dot(o_ref[...], w_ref[...], preferred_element_type=F32)
    x_new = x_ref[...] + gate_ref[...] * y
    _epilogue(x_new, g2_ref[...], sh2_ref[...], sc2_ref[...], wr_ref[...], xo_ref, h2_ref, aff_ref)


def _epilogue_specs(d, tm, blocks_per_mod):
    mod_spec = pl.BlockSpec((None, 1, d), lambda i: (i // blocks_per_mod, 0, 0))
    in_specs = [pl.BlockSpec((1, d), lambda i: (0, 0)), mod_spec, mod_spec,
                pl.BlockSpec((N_EXPERTS, d), lambda i: (0, 0))]
    out_specs = [pl.BlockSpec((tm, d), lambda i: (i, 0)),
                 pl.BlockSpec((tm, d), lambda i: (i, 0)),
                 pl.BlockSpec((tm // LANES, N_EXPERTS, LANES), lambda i: (i, 0, 0))]
    return in_specs, out_specs


def _epilogue_out_shape(n, d):
    return [jax.ShapeDtypeStruct((n, d), F32),
            jax.ShapeDtypeStruct((n, d), F32),
            jax.ShapeDtypeStruct((n // LANES, N_EXPERTS, LANES), F32)]


def _proj_mixer(o, w_bf16, x2d, gate1, g2, sh2, sc2, wr_t, tokens_per_mod):
    n, d = x2d.shape
    tm = ROW_TILE
    kdim = o.shape[1]
    bpm = tokens_per_mod // tm
    ep_in, ep_out = _epilogue_specs(d, tm, bpm)
    return pl.pallas_call(
        _proj_kernel,
        grid=(n // tm,),
        in_specs=[pl.BlockSpec((tm, kdim), lambda i: (i, 0)),
                  pl.BlockSpec((kdim, d), lambda i: (0, 0)),
                  pl.BlockSpec((tm, d), lambda i: (i, 0)),
                  pl.BlockSpec((None, 1, d), lambda i: (i // bpm, 0, 0))] + ep_in,
        out_specs=ep_out,
        out_shape=_epilogue_out_shape(n, d),
        compiler_params=_cparams(("parallel",)),
        name="proj_mixer",
    )(o, w_bf16, x2d, gate1, g2.reshape(1, d), sh2, sc2, wr_t)


def _pool_kernel(x_ref, xprev_ref, xnext_ref, g1_ref, sh1_ref, sc1_ref, pw_ref, pb_ref, ps_ref,
                 gate_ref, g2_ref, sh2_ref, sc2_ref, wr_ref, xo_ref, h2_ref, aff_ref, *, seq_len):
    tm, d = x_ref.shape
    halo = POOL_HALO
    i = pl.program_id(0)
    pos0 = (i * tm) % seq_len
    g1, sh1, sc1 = g1_ref[...], sh1_ref[...], sc1_ref[...]
    x = x_ref[...]
    h = _adaln_rows(x, g1, sh1, sc1)
    hprev = _adaln_rows(xprev_ref[...], g1, sh1, sc1)
    hnext = _adaln_rows(xnext_ref[...], g1, sh1, sc1)
    hprev = jnp.where(pos0 > 0, hprev, 0.0)
    hnext = jnp.where(pos0 + tm < seq_len, hnext, 0.0)
    hext = jnp.concatenate([hprev, h, hnext], axis=0)
    rows = tm + 2 * halo
    t = pos0 + lax.broadcasted_iota(jnp.int32, (tm, 1), 0)
    outs = []
    for gi, win in enumerate(POOL_WINDOWS):
        a = hext[:, gi * POOL_GROUP:(gi + 1) * POOL_GROUP]
        p = pltpu.roll(a, 1, axis=0) + a
        step = 1
        while 2 * step < win:
            p = pltpu.roll(p, step, axis=0) + pltpu.roll(p, rows - step, axis=0)
            step *= 2
        half = win // 2
        cnt = (jnp.minimum(t + half, seq_len) - jnp.maximum(t - half, 0)).astype(F32)
        pooled = p[halo:halo + tm, :] / cnt - a[halo:halo + tm, :]
        y = jnp.dot(pooled.astype(BF16), pw_ref[gi].astype(BF16), preferred_element_type=F32)
        outs.append(y + pb_ref[gi:gi + 1, :])
    mix = jnp.concatenate(outs, axis=1) * ps_ref[...]
    x_new = x + gate_ref[...] * mix
    _epilogue(x_new, g2_ref[...], sh2_ref[...], sc2_ref[...], wr_ref[...], xo_ref, h2_ref, aff_ref)


def _pool_mixer(x2d, g1, sh1, sc1, pool_w, pool_b, pool_scale, gate1, g2, sh2, sc2, wr_t,
                tokens_per_mod, seq_len):
    n, d = x2d.shape
    tm = min(ROW_TILE, seq_len)
    bpm = tokens_per_mod // tm
    hb = tm // POOL_HALO
    n_hblocks = n // POOL_HALO
    mod_spec = pl.BlockSpec((None, 1, d), lambda i: (i // bpm, 0, 0))
    ep_in, ep_out = _epilogue_specs(d, tm, bpm)
    return pl.pallas_call(
        functools.partial(_pool_kernel, seq_len=seq_len),
        grid=(n // tm,),
        in_specs=[pl.BlockSpec((tm, d), lambda i: (i, 0)),
                  pl.BlockSpec((POOL_HALO, d), lambda i: (jnp.maximum(i * hb - 1, 0), 0)),
                  pl.BlockSpec((POOL_HALO, d),
                               lambda i: (jnp.minimum((i + 1) * hb, n_hblocks - 1), 0)),
                  pl.BlockSpec((1, d), lambda i: (0, 0)), mod_spec, mod_spec,
                  pl.BlockSpec((N_POOL_GROUPS, POOL_GROUP, POOL_GROUP), lambda i: (0, 0, 0)),
                  pl.BlockSpec((N_POOL_GROUPS, POOL_GROUP), lambda i: (0, 0)),
                  pl.BlockSpec((1, d), lambda i: (0, 0)),
                  mod_spec] + ep_in,
        out_specs=ep_out,
        out_shape=_epilogue_out_shape(n, d),
        compiler_params=_cparams(("parallel",)),
        name="pool_mixer",
    )(x2d, x2d, x2d, g1.reshape(1, d), sh1, sc1, pool_w, pool_b, pool_scale.reshape(1, d),
      gate1, g2.reshape(1, d), sh2, sc2, wr_t)


def _ffn_kernel(idx_ref, h2a_hbm, h2b_hbm, wg_ref, wu_ref, wd_ref, g_ref, ye_ref, xbuf, sem):
    n_e, n_s = pl.num_programs(0), pl.num_programs(1)
    ei, si = pl.program_id(0), pl.program_id(1)
    n_blocks, sub = xbuf.shape[1], xbuf.shape[2]
    cap = n_blocks * sub
    rows = min(FFN_ROWS, cap)
    step = ei * n_s + si
    slot = step % 2

    def issue(e_t, s_t, slot_t):
        base = (s_t * n_e + e_t) * cap
        for set_id, src in enumerate((h2a_hbm, h2b_hbm)):
            @pl.when(s_t == set_id)
            def _(src=src):
                def body(kb, carry):
                    k0 = kb * sub
                    for u in range(sub):
                        pltpu.async_copy(src.at[pl.ds(idx_ref[base + k0 + u], 1)],
                                         xbuf.at[slot_t, kb, pl.ds(u, 1)], sem.at[slot_t],
                                         priority=u % 2)
                    return carry

                lax.fori_loop(0, n_blocks, body, 0)

    @pl.when(step == 0)
    def _():
        issue(ei, si, slot)

    @pl.when(step + 1 < n_e * n_s)
    def _():
        nxt = step + 1
        issue(nxt // n_s, nxt % n_s, 1 - slot)

    def wait_body(kb, carry):
        pltpu.make_async_copy(h2a_hbm.at[pl.ds(0, sub)], xbuf.at[slot, 0], sem.at[slot]).wait()
        return carry

    lax.fori_loop(0, n_blocks, wait_body, 0)

    def body(i, carry):
        r0 = pl.multiple_of(i * rows, rows)
        b0 = pl.multiple_of(i * (rows // sub), rows // sub)
        x = xbuf[slot, pl.ds(b0, rows // sub)].reshape(rows, xbuf.shape[3]).astype(BF16)
        a = jnp.dot(x, wg_ref[...], preferred_element_type=F32)
        b = jnp.dot(x, wu_ref[...], preferred_element_type=F32)
        hid = (a * (1.0 / (1.0 + jnp.exp(-a))) * b).astype(BF16)
        y = jnp.dot(hid, wd_ref[...], preferred_element_type=F32)
        gate = g_ref[pl.ds(r0, rows), :]
        for j in range(y.shape[1] // LANES):
            ye_ref[pl.ds(r0, rows), j * LANES:(j + 1) * LANES] = (
                y[:, j * LANES:(j + 1) * LANES] * gate).astype(ye_ref.dtype)
        return carry

    lax.fori_loop(0, cap // rows, body, 0)


def _expert_ffn(idx_flat, h2a, h2b, wg, wu, wd, gates, layer):
    s, e, c, _ = gates.shape
    assert s == 2
    d, f = wg.shape[2], wg.shape[3]
    return pl.pallas_call(
        _ffn_kernel,
        grid_spec=pltpu.PrefetchScalarGridSpec(
            num_scalar_prefetch=1,
            grid=(e, s),
            in_specs=[pl.BlockSpec(memory_space=pl.ANY),
                      pl.BlockSpec(memory_space=pl.ANY),
                      pl.BlockSpec((None, None, d, f), lambda ei, si, idx: (layer, ei, 0, 0)),
                      pl.BlockSpec((None, None, d, f), lambda ei, si, idx: (layer, ei, 0, 0)),
                      pl.BlockSpec((None, None, f, d), lambda ei, si, idx: (layer, ei, 0, 0)),
                      pl.BlockSpec((None, None, c, LANES), lambda ei, si, idx: (si, ei, 0, 0))],
            out_specs=pl.BlockSpec((None, None, c, d), lambda ei, si, idx: (si, ei, 0, 0)),
            scratch_shapes=[pltpu.VMEM((2, c // SUBLANES, SUBLANES, d), F32),
                            pltpu.SemaphoreType.DMA((2,))]),
        out_shape=jax.ShapeDtypeStruct((s, e, c, d), BF16),
        compiler_params=pltpu.CompilerParams(dimension_semantics=("arbitrary", "arbitrary"),
                                             vmem_limit_bytes=VMEM_LIMIT,
                                             disable_bounds_checks=True),
        name="expert_ffn",
    )(idx_flat, h2a, h2b, wg, wu, wd, gates)


def _combine_kernel(clo_ref, ncnt_ref, ye_hbm, tok_hbm, x_ref, gate_ref, fg_ref, o_ref,
                    zbuf, tokbuf, acc_ref, sem, *, final, set_id):
    n_e = ye_hbm.shape[1]
    tm, d = x_ref.shape
    ch, grp = COMBINE_CH, COMBINE_GROUP
    i = pl.program_id(0)
    nt = pl.num_programs(0)
    slot = i % 2

    def chunk_copies(e, src_row, slot_t, dst_row):
        return (pltpu.make_async_copy(ye_hbm.at[set_id, e, pl.ds(src_row, ch), :],
                                      zbuf.at[slot_t, pl.ds(dst_row, ch), :], sem.at[slot_t]),
                pltpu.make_async_copy(tok_hbm.at[set_id, e, pl.ds(src_row, ch), :],
                                      tokbuf.at[slot_t, pl.ds(dst_row, ch), :], sem.at[slot_t]))

    def issue(tile, slot_t):
        pos = 0
        for e in range(n_e):
            lo = clo_ref[tile * n_e + e]

            def body(j, p, e=e, lo=lo):
                src = pl.multiple_of((lo + j) * ch, ch)
                dst = pl.multiple_of(p * ch, ch)
                for cp in chunk_copies(e, src, slot_t, dst):
                    cp.start()
                return p + 1

            pos = lax.fori_loop(0, ncnt_ref[tile * n_e + e], body, pos)

    def total_chunks(tile):
        tot = 0
        for e in range(n_e):
            tot = tot + ncnt_ref[tile * n_e + e]
        return tot

    @pl.when(i == 0)
    def _():
        zbuf[...] = jnp.zeros(zbuf.shape, zbuf.dtype)
        tokbuf[...] = jnp.full(tokbuf.shape, -1, jnp.int32)
        issue(i, slot)

    @pl.when(i + 1 < nt)
    def _():
        issue(i + 1, 1 - slot)

    total = total_chunks(i)

    def wait_body(j, carry):
        for cp in chunk_copies(0, 0, slot, 0):
            cp.wait()
        return carry

    lax.fori_loop(0, total, wait_body, 0)

    acc_ref[...] = jnp.zeros(acc_ref.shape, F32)
    tok_of_lane = i * tm + lax.broadcasted_iota(jnp.int32, (grp, tm), 1)
    row_in_group = lax.broadcasted_iota(jnp.int32, (grp, 1), 0)

    def group_body(gidx, carry):
        r0 = pl.multiple_of(gidx * grp, grp)
        tok = jnp.where(r0 + row_in_group < total * ch, tokbuf[slot, pl.ds(r0, grp), :], -1)
        tok = jnp.tile(tok, (1, tm // LANES))
        onehot_t = (tok == tok_of_lane).astype(BF16)
        acc_ref[...] += lax.dot_general(onehot_t, zbuf[slot, pl.ds(r0, grp), :],
                                        (((0,), (0,)), ((), ())), preferred_element_type=F32)
        return carry

    lax.fori_loop(0, (total * ch + grp - 1) // grp, group_body, 0)

    x = x_ref[...] + gate_ref[...] * acc_ref[...]
    if final:
        ms = jnp.mean(x * x, axis=-1, keepdims=True)
        x = x * lax.rsqrt(ms + EPS) * fg_ref[...]
    o_ref[...] = x


def _combine(ye, tok_col, toff, set_id, x2d, gate2, final_g, tokens_per_mod, final):
    n, d = x2d.shape
    _, e, c, _ = ye.shape
    tm, ch = COMBINE_TM, COMBINE_CH
    n_tiles = n // tm
    bpm = tokens_per_mod // tm
    start = toff[::tm // LANES, :, 0]
    first = jnp.concatenate([start, jnp.full((1, e), c, jnp.int32)], axis=0)
    clo = first[:-1] // ch
    chi = (first[1:] + ch - 1) // ch
    ncnt = jnp.maximum(chi - clo, 0)
    max_rows = e * (tm // ch + 2) * ch
    max_rows = ((max_rows + COMBINE_GROUP - 1) // COMBINE_GROUP) * COMBINE_GROUP
    return pl.pallas_call(
        functools.partial(_combine_kernel, final=final, set_id=set_id),
        grid_spec=pltpu.PrefetchScalarGridSpec(
            num_scalar_prefetch=2,
            grid=(n_tiles,),
            in_specs=[pl.BlockSpec(memory_space=pl.ANY),
                      pl.BlockSpec(memory_space=pl.ANY),
                      pl.BlockSpec((tm, d), lambda i, a, b: (i, 0)),
                      pl.BlockSpec((None, 1, d), lambda i, a, b: (i // bpm, 0, 0)),
                      pl.BlockSpec((1, d), lambda i, a, b: (0, 0))],
            out_specs=pl.BlockSpec((tm, d), lambda i, a, b: (i, 0)),
            scratch_shapes=[pltpu.VMEM((2, max_rows, d), BF16),
                            pltpu.VMEM((2, max_rows, LANES), jnp.int32),
                            pltpu.VMEM((tm, d), F32),
                            pltpu.SemaphoreType.DMA((2,))]),
        out_shape=jax.ShapeDtypeStruct((n, d), F32),
        compiler_params=_cparams(("arbitrary",)),
        name="combine_final" if final else "combine",
    )(clo.reshape(-1), ncnt.reshape(-1), ye, tok_col, x2d, gate2, final_g.reshape(1, d))


def _rope_tables(n_tokens):
    t = jnp.arange(n_tokens)
    row = (t // GRID_W).astype(F32)
    col = (t % GRID_W).astype(F32)
    half = HEAD_DIM // 2
    inv = ROPE_BASE ** (-jnp.arange(0, half, 2, dtype=F32) / half)
    ar, ac = row[:, None] * inv, col[:, None] * inv
    c64 = jnp.concatenate([jnp.cos(ar), jnp.cos(ar), jnp.cos(ac), jnp.cos(ac)], axis=-1)
    s64 = jnp.concatenate([-jnp.sin(ar), jnp.sin(ar), -jnp.sin(ac), jnp.sin(ac)], axis=-1)
    return jnp.tile(c64, (1, 2)), jnp.tile(s64, (1, 2))


def _tile_prefix(mask3, il_ref, to_ref, tot_ref):
    nt, ne, ln = mask3.shape
    m2 = mask3.astype(F32).astype(BF16).reshape(nt * ne, ln)
    r = lax.broadcasted_iota(jnp.int32, (ln, ln), 0)
    c = lax.broadcasted_iota(jnp.int32, (ln, ln), 1)
    upper = (r <= c).astype(F32).astype(BF16)
    ones = jnp.ones((ln, ln), BF16)
    il_ref[...] = jnp.dot(m2, upper, preferred_element_type=F32).reshape(nt, ne, ln)
    tot_ref[...] = jnp.dot(m2, ones, preferred_element_type=F32).reshape(nt, ne, ln)
    run = jnp.zeros((ne, ln), F32)
    for tt in range(nt):
        to_ref[tt] = run
        run = run + tot_ref[tt]


def _split3_bf16(v):
    p1 = v.astype(BF16)
    r1 = v - p1.astype(F32)
    p2 = r1.astype(BF16)
    return p1, p2, (r1 - p2.astype(F32)).astype(BF16)


def _route_kernel(aff_ref, idx_ref, gcol_ref, tokcol_ref, toff_ref, il_ref, to_ref, tot_ref, *, cap):
    nt, ne, ln = aff_ref.shape
    a3 = aff_ref[...]

    def count(mask3):
        per_lane = jnp.sum(mask3.astype(jnp.int32), axis=0)
        return jnp.sum(per_lane, axis=1, keepdims=True)

    def as_float(thr_bits):
        return pltpu.bitcast(jnp.broadcast_to(thr_bits, (ne, ln)), F32)[:, 0:1]

    def search(i, thr_bits):
        cand = thr_bits | lax.shift_left(jnp.int32(1), 30 - i)
        return jnp.where(count(a3 >= as_float(cand)[None]) >= cap, cand, thr_bits)

    thr = as_float(lax.fori_loop(0, 31, search, jnp.zeros((ne, 1), jnp.int32)))
    gt = a3 > thr[None]
    eq = a3 == thr[None]
    need = (cap - count(gt)).astype(F32)
    _tile_prefix(eq, il_ref, to_ref, tot_ref)
    eq_before = to_ref[...] + il_ref[...] - eq.astype(F32)
    sel = gt | (eq & (eq_before < need[None]))
    _tile_prefix(sel, il_ref, to_ref, tot_ref)
    toff_ref[...] = to_ref[...].astype(jnp.int32)

    reps = cap // ln
    ones = jnp.ones((ln, ln), BF16)

    def to_column(vals, bf16_exact=False):
        out = None
        for piece in ((vals.astype(BF16),) if bf16_exact else _split3_bf16(vals)):
            t = lax.dot_general(piece, ones, (((0,), (0,)), ((), ())), preferred_element_type=F32)
            out = t if out is None else out + t
        return out

    slot = lax.broadcasted_iota(jnp.int32, (nt, cap), 1).astype(F32)
    tile_id = lax.broadcasted_iota(jnp.int32, (nt, cap), 0).astype(F32)
    in_tile = lax.broadcasted_iota(jnp.int32, (ln, cap), 0).astype(F32)
    for e in range(ne):
        il_e = il_ref[:, e, :]
        to_e = jnp.tile(to_ref[:, e, :], (1, reps))
        tot_e = jnp.tile(tot_ref[:, e, :], (1, reps))
        owns = (to_e <= slot) & (slot < to_e + tot_e)
        rank = jnp.sum(jnp.where(owns, slot - to_e, 0.0), axis=0, keepdims=True)
        tile_of = jnp.sum(jnp.where(owns, tile_id, 0.0), axis=0, keepdims=True)
        owns_b = owns.astype(F32).astype(BF16)
        contract0 = (((0,), (0,)), ((), ()))
        il_g = lax.dot_general(il_e.astype(BF16), owns_b, contract0,
                               preferred_element_type=F32)
        pos = jnp.sum((il_g <= rank).astype(F32), axis=0, keepdims=True)
        idx_ref[e:e + 1, :] = (tile_of * ln + pos).astype(jnp.int32)
        p1, p2, p3 = _split3_bf16(aff_ref[:, e, :])
        a_g = (lax.dot_general(p1, owns_b, contract0, preferred_element_type=F32)
               + lax.dot_general(p2, owns_b, contract0, preferred_element_type=F32)
               + lax.dot_general(p3, owns_b, contract0, preferred_element_type=F32))
        hit = in_tile == pos
        gcol_ref[e] = to_column(jnp.where(hit, a_g, 0.0))
        tok_hi = to_column(jnp.where(hit, tile_of, 0.0), bf16_exact=True)
        tok_lo = to_column(jnp.where(hit, pos, 0.0), bf16_exact=True)
        tokcol_ref[e] = (tok_hi * ln + tok_lo).astype(jnp.int32)


def _route(aff_all, cap):
    s, nt, ne, ln = aff_all.shape
    assert nt <= 256 and cap % ln == 0
    return pl.pallas_call(
        functools.partial(_route_kernel, cap=cap),
        grid=(s,),
        in_specs=[pl.BlockSpec((None, nt, ne, ln), lambda i: (i, 0, 0, 0))],
        out_specs=[pl.BlockSpec((None, ne, cap), lambda i: (i, 0, 0)),
                   pl.BlockSpec((None, ne, cap, ln), lambda i: (i, 0, 0, 0)),
                   pl.BlockSpec((None, ne, cap, ln), lambda i: (i, 0, 0, 0)),
                   pl.BlockSpec((None, nt, ne, ln), lambda i: (i, 0, 0, 0))],
        out_shape=[jax.ShapeDtypeStruct((s, ne, cap), jnp.int32),
                   jax.ShapeDtypeStruct((s, ne, cap, ln), F32),
                   jax.ShapeDtypeStruct((s, ne, cap, ln), jnp.int32),
                   jax.ShapeDtypeStruct((s, nt, ne, ln), jnp.int32)],
        scratch_shapes=[pltpu.VMEM((nt, ne, ln), F32)] * 3,
        compiler_params=_cparams(("parallel",)),
        name="route",
    )(aff_all)


def _route_and_ffn(h2_list, aff_list, w_gate, w_up, w_down, layer):
    n = h2_list[0].shape[0]
    assert all(h.shape[0] == n for h in h2_list)
    cap = EC_CAPACITY * n // N_EXPERTS
    idx_all, gcol, tokcol, toff = _route(jnp.stack(aff_list), cap)
    ye = _expert_ffn(idx_all.reshape(-1), h2_list[0], h2_list[1], w_gate, w_up, w_down, gcol, layer)
    return ye, tokcol, toff


def kernel(x_prompt, x_sample, cache_k, cache_v, c, c_ctx, w_mod, b_mod, norm1_g, norm2_g,
           attn_wqkv, attn_wo, lambda_q1, lambda_k1, lambda_q2, lambda_k2, subln_g,
           pool_w, pool_b, pool_scale, router_w, expert_w_gate, expert_w_up, expert_w_down,
           final_g):
    bp, n_ctx, d = x_prompt.shape
    bs, n_lat, _ = x_sample.shape
    past = cache_k.shape[2]
    n_p, n_s = bp * n_ctx, bs * n_lat

    rows = 8 * ((1 + bs + 7) // 8)
    cvecs = jnp.zeros((rows, d), F32).at[0].set(c_ctx).at[1:1 + bs].set(c)
    mod = _modulation_all(cvecs, w_mod, b_mod)
    rope_tabs = _rope_tables(n_lat)
    cache_k4 = cache_k.reshape(bs, cache_k.shape[1], past, ATTN_WIDTH)
    cache_v4 = cache_v.reshape(bs, cache_v.shape[1], past, ATTN_WIDTH)
    w_gate_b = expert_w_gate.astype(BF16)
    w_up_b = expert_w_up.astype(BF16)
    w_down_b = expert_w_down.astype(BF16)

    xp = x_prompt.reshape(n_p, d)
    xs = x_sample.reshape(n_s, d)
    new_k, new_v = [], []
    for l in range(DEPTH):
        mp_ = mod[l, 0:1].reshape(1, 6, 1, d)
        ms_ = mod[l, 1:1 + bs].reshape(bs, 6, 1, d)
        sp1, cp1, gp1, sp2, cp2, gp2 = [mp_[:, i] for i in range(6)]
        ss1, cs1, gs1, ss2, cs2, gs2 = [ms_[:, i] for i in range(6)]
        wr_t = router_w[l].T.astype(BF16)
        if l % 2 == 0:
            a = l // 2
            lam_init = 0.8 - 0.6 * math.exp(-0.3 * l)
            lamv = jnp.stack([lambda_q1[a], lambda_k1[a], lambda_q2[a], lambda_k2[a]])
            wqkv = attn_wqkv[a].astype(BF16)
            wo = attn_wo[a].astype(BF16)
            qp, kp, vp = _qkv_project(xp, norm1_g[l], sp1, cp1, wqkv, None, n_p, F32)
            new_k.append(kp.reshape(bp, n_ctx, N_HEADS, 2 * HEAD_DIM))
            new_v.append(vp.reshape(bp, n_ctx, N_HEADS, V_DIM))
            op = _diff_attention(qp, kp, vp, lamv, subln_g[a], lam_init, bp)
            xp, h2p, affp = _proj_mixer(op, wo, xp, gp1, norm2_g[l], sp2, cp2, wr_t, n_p)
            qs, ks_, vs = _qkv_project(xs, norm1_g[l], ss1, cs1, wqkv, rope_tabs, n_lat, BF16)
            os_ = _diff_attention(qs, ks_, vs, lamv, subln_g[a], lam_init, bs,
                                  cache=(cache_k4, cache_v4, a))
            xs, h2s, affs = _proj_mixer(os_, wo, xs, gs1, norm2_g[l], ss2, cs2, wr_t, n_lat)
        else:
            p = l // 2
            xp, h2p, affp = _pool_mixer(xp, norm1_g[l], sp1, cp1, pool_w[p], pool_b[p],
                                        pool_scale[p], gp1, norm2_g[l], sp2, cp2, wr_t,
                                        n_p, n_ctx)
            xs, h2s, affs = _pool_mixer(xs, norm1_g[l], ss1, cs1, pool_w[p], pool_b[p],
                                        pool_scale[p], gs1, norm2_g[l], ss2, cs2, wr_t,
                                        n_lat, n_lat)
        ye, tokcol, toff = _route_and_ffn([h2p, h2s], [affp, affs], w_gate_b, w_up_b, w_down_b, l)
        last = l == DEPTH - 1
        xp = _combine(ye, tokcol, toff[0], 0, xp, gp2, final_g, n_p, last)
        xs = _combine(ye, tokcol, toff[1], 1, xs, gs2, final_g, n_lat, last)
    return (xp.reshape(bp, n_ctx, d), xs.reshape(bs, n_lat, d),
            jnp.stack(new_k, axis=1), jnp.stack(new_v, axis=1))
```

```python
import math
import functools
import jax
import jax.numpy as jnp
from jax import lax
from jax.experimental import pallas as pl
from jax.experimental.pallas import tpu as pltpu

D_MODEL = 1024
DEPTH = 4
GRID_W = 64
N_HEADS = 8
HEAD_DIM = 64
V_DIM = 2 * HEAD_DIM
ATTN_WIDTH = N_HEADS * V_DIM
ROPE_BASE = 10000.0
N_POOL_GROUPS = 4
POOL_GROUP = D_MODEL // N_POOL_GROUPS
POOL_WINDOWS = (2, 4, 8, 16)
POOL_HALO = 8
N_EXPERTS = 16
EC_CAPACITY = 2
EPS = 1e-6

F32 = jnp.float32
BF16 = jnp.bfloat16
LANES = 128
LOG2E = 1.4426950408889634
VMEM_LIMIT = 56 * 1024 * 1024
ROW_TILE = 512
ATTN_TQ = 512
ATTN_TK = 4096
ATTN_HEADS_PER_STEP = 2
FFN_ROWS = 512
SUBLANES = 8
COMBINE_TM = 256
COMBINE_CH = 32
COMBINE_GROUP = 256


def _cparams(sem):
    return pltpu.CompilerParams(dimension_semantics=sem, vmem_limit_bytes=VMEM_LIMIT)


def _mod_kernel(c_ref, w_ref, b_ref, o_ref):
    c = c_ref[...]
    s = (c * (1.0 / (1.0 + jnp.exp(-c)))).astype(BF16)
    o_ref[...] = jnp.dot(s, w_ref[...].astype(BF16), preferred_element_type=F32) + b_ref[...]


def _modulation_all(cvecs, w_mod, b_mod):
    r, d = cvecs.shape
    depth, _, n6 = w_mod.shape
    tn = 1024
    return pl.pallas_call(
        _mod_kernel,
        grid=(depth, n6 // tn),
        in_specs=[pl.BlockSpec((r, d), lambda l, j: (0, 0)),
                  pl.BlockSpec((None, d, tn), lambda l, j: (l, 0, j)),
                  pl.BlockSpec((None, 1, tn), lambda l, j: (l, 0, j))],
        out_specs=pl.BlockSpec((None, r, tn), lambda l, j: (l, 0, j)),
        out_shape=jax.ShapeDtypeStruct((depth, r, n6), F32),
        compiler_params=_cparams(("parallel", "parallel")),
        name="modulation",
    )(cvecs, w_mod, b_mod.reshape(depth, 1, n6))


def _adaln_rows(x, g, shift, scale):
    ms = jnp.mean(x * x, axis=-1, keepdims=True)
    return (x * lax.rsqrt(ms + EPS)) * g * (1.0 + scale) + shift


def _swap16(x):
    lane = lax.broadcasted_iota(jnp.int32, x.shape, 1)
    up = pltpu.roll(x, x.shape[1] - 16, axis=1)
    dn = pltpu.roll(x, 16, axis=1)
    return jnp.where((lane % 32) < 16, up, dn)


def _qkv_kernel(*refs, rope):
    if rope:
        x_ref, g_ref, sh_ref, sc_ref, w_ref, cos_ref, sin_ref, q_ref, k_ref, v_ref = refs
    else:
        x_ref, g_ref, sh_ref, sc_ref, w_ref, q_ref, k_ref, v_ref = refs
    h = _adaln_rows(x_ref[...], g_ref[...], sh_ref[...], sc_ref[...]).astype(BF16)
    width = q_ref.shape[1]
    qscale = (HEAD_DIM ** -0.5) * LOG2E
    for part, o_ref in enumerate((q_ref, k_ref, v_ref)):
        r = jnp.dot(h, w_ref[:, part * width:(part + 1) * width], preferred_element_type=F32)
        if part < 2 and rope:
            c = cos_ref[...]
            s = sin_ref[...]
            for hd in range(width // LANES):
                blk = r[:, hd * LANES:(hd + 1) * LANES]
                blk = blk * c + _swap16(blk) * s
                if part == 0:
                    blk = blk * qscale
                o_ref[:, hd * LANES:(hd + 1) * LANES] = blk.astype(o_ref.dtype)
        else:
            if part == 0:
                r = r * qscale
            o_ref[...] = r.astype(o_ref.dtype)


def _qkv_project(x2d, g, shift, scale, w_bf16, rope_tabs, tokens_per_mod, kv_dtype):
    n, d = x2d.shape
    tm = ROW_TILE
    width = w_bf16.shape[1] // 3
    rope = rope_tabs is not None
    blocks_per_mod = tokens_per_mod // tm
    in_specs = [pl.BlockSpec((tm, d), lambda i: (i, 0)),
                pl.BlockSpec((1, d), lambda i: (0, 0)),
                pl.BlockSpec((None, 1, d), lambda i: (i // blocks_per_mod, 0, 0)),
                pl.BlockSpec((None, 1, d), lambda i: (i // blocks_per_mod, 0, 0)),
                pl.BlockSpec((d, 3 * width), lambda i: (0, 0))]
    args = [x2d, g.reshape(1, d), shift, scale, w_bf16]
    if rope:
        cos_t, sin_t = rope_tabs
        seq_blocks = cos_t.shape[0] // tm
        in_specs += [pl.BlockSpec((tm, LANES), lambda i: (i % seq_blocks, 0)),
                     pl.BlockSpec((tm, LANES), lambda i: (i % seq_blocks, 0))]
        args += [cos_t, sin_t]
    out_spec = pl.BlockSpec((tm, width), lambda i: (i, 0))
    return pl.pallas_call(
        functools.partial(_qkv_kernel, rope=rope),
        grid=(n // tm,),
        in_specs=in_specs,
        out_specs=[out_spec, out_spec, out_spec],
        out_shape=[jax.ShapeDtypeStruct((n, width), BF16),
                   jax.ShapeDtypeStruct((n, width), kv_dtype),
                   jax.ShapeDtypeStruct((n, width), kv_dtype)],
        compiler_params=_cparams(("parallel",)),
        name="adaln_qkv_rope" if rope else "adaln_qkv",
    )(*args)


def _attn_kernel(*refs, tk, lam_init, hp, cached):
    if cached:
        (lamv_ref, q_ref, k_ref, v_ref, kc_ref, vc_ref, g_ref, o_ref,
         qs_ref, m_ref, l_ref, acc_ref) = refs
    else:
        lamv_ref, q_ref, k_ref, v_ref, g_ref, o_ref, qs_ref, m_ref, l_ref, acc_ref = refs
    tq = q_ref.shape[0]
    nk = k_ref.shape[0]
    for h in range(hp):
        q = q_ref[:, h * LANES:(h + 1) * LANES]
        lane = lax.broadcasted_iota(jnp.int32, q.shape, 1)
        zero = jnp.zeros_like(q)
        qs_ref[h, 0:tq, :] = jnp.where(lane < HEAD_DIM, q, zero)
        qs_ref[h, tq:2 * tq, :] = jnp.where(lane >= HEAD_DIM, q, zero)
    m_ref[...] = jnp.full(m_ref.shape, -jnp.inf, F32)
    l_ref[...] = jnp.zeros(l_ref.shape, F32)
    acc_ref[...] = jnp.zeros(acc_ref.shape, F32)

    def online_softmax_step(h, kc, vc):
        s = lax.dot_general(qs_ref[h], kc, (((1,), (1,)), ((), ())),
                            preferred_element_type=F32)
        m_old = m_ref[h]
        m_new = jnp.maximum(m_old, jnp.max(s, axis=1)[:, None])
        alpha = jnp.exp2(m_old - m_new)
        p = jnp.exp2(s - jnp.tile(m_new, (1, s.shape[1] // LANES)))
        l_ref[h] = alpha * l_ref[h] + jnp.sum(p, axis=1)[:, None]
        acc_ref[h] = alpha * acc_ref[h] + jnp.dot(p.astype(BF16), vc, preferred_element_type=F32)
        m_ref[h] = m_new

    def body(j, carry):
        off = pl.multiple_of(j * tk, tk)
        for h in range(hp):
            online_softmax_step(h, k_ref[pl.ds(off, tk), h * LANES:(h + 1) * LANES].astype(BF16),
                                v_ref[pl.ds(off, tk), h * LANES:(h + 1) * LANES].astype(BF16))
        return carry

    lax.fori_loop(0, nk // tk, body, 0)
    if cached:
        for h in range(hp):
            online_softmax_step(h, kc_ref[:, h * LANES:(h + 1) * LANES].astype(BF16),
                                vc_ref[:, h * LANES:(h + 1) * LANES].astype(BF16))

    lv = lamv_ref[...]
    lam = (jnp.exp(jnp.sum(lv[0:1, :] * lv[1:2, :], axis=-1, keepdims=True))
           - jnp.exp(jnp.sum(lv[2:3, :] * lv[3:4, :], axis=-1, keepdims=True)) + lam_init)
    for h in range(hp):
        o = acc_ref[h] / l_ref[h]
        d = o[0:tq, :] - lam * o[tq:2 * tq, :]
        ms = jnp.mean(d * d, axis=-1, keepdims=True)
        o_ref[:, h * LANES:(h + 1) * LANES] = (
            (d * lax.rsqrt(ms + EPS)) * g_ref[...] * (1.0 - lam_init)).astype(o_ref.dtype)


def _diff_attention(q, k, v, lamv, subln_g, lam_init, batch, cache=None):
    nq = q.shape[0] // batch
    nk = k.shape[0] // batch
    tq = min(ATTN_TQ, nq)
    tk = min(ATTN_TK, nk)
    assert nq % tq == 0 and nk % tk == 0 and tk % LANES == 0
    hp = ATTN_HEADS_PER_STEP
    w = hp * LANES
    q3 = q.reshape(batch, nq, ATTN_WIDTH)
    k3 = k.reshape(batch, nk, ATTN_WIDTH)
    v3 = v.reshape(batch, nk, ATTN_WIDTH)
    kv_spec = pl.BlockSpec((None, nk, w), lambda b, h, i: (b, 0, h))
    in_specs = [pl.BlockSpec((4, HEAD_DIM), lambda b, h, i: (0, 0)),
                pl.BlockSpec((None, tq, w), lambda b, h, i: (b, i, h)), kv_spec, kv_spec]
    args = [lamv, q3, k3, v3]
    if cache is not None:
        ck, cv, layer = cache
        past = ck.shape[2]
        assert past % LANES == 0
        c_spec = pl.BlockSpec((None, None, past, w), lambda b, h, i: (b, layer, 0, h))
        in_specs += [c_spec, c_spec]
        args += [ck, cv]
    in_specs.append(pl.BlockSpec((1, V_DIM), lambda b, h, i: (0, 0)))
    args.append(subln_g.reshape(1, V_DIM))
    out = pl.pallas_call(
        functools.partial(_attn_kernel, tk=tk, lam_init=lam_init, hp=hp, cached=cache is not None),
        grid=(batch, N_HEADS // hp, nq // tq),
        in_specs=in_specs,
        out_specs=pl.BlockSpec((None, tq, w), lambda b, h, i: (b, i, h)),
        out_shape=jax.ShapeDtypeStruct((batch, nq, ATTN_WIDTH), BF16),
        scratch_shapes=[pltpu.VMEM((hp, 2 * tq, V_DIM), BF16),
                        pltpu.VMEM((hp, 2 * tq, LANES), F32),
                        pltpu.VMEM((hp, 2 * tq, LANES), F32),
                        pltpu.VMEM((hp, 2 * tq, V_DIM), F32)],
        compiler_params=_cparams(("parallel", "parallel", "parallel")),
        name="diff_attention",
    )(*args)
    return out.reshape(batch * nq, ATTN_WIDTH)


def _epilogue(x_new, g2, sh2, sc2, wr_t, xo_ref, h2_ref, aff_ref):
    xo_ref[...] = x_new
    h2 = _adaln_rows(x_new, g2, sh2, sc2)
    h2b = h2.astype(BF16)
    h2_ref[...] = h2
    logits = lax.dot_general(wr_t, h2b, (((1,), (1,)), ((), ())),
                             preferred_element_type=F32)
    mx = jnp.max(logits, axis=0, keepdims=True)
    ex = jnp.exp(logits - mx)
    aff = ex / jnp.sum(ex, axis=0, keepdims=True)
    for r in range(aff_ref.shape[0]):
        aff_ref[r] = aff[:, r * LANES:(r + 1) * LANES]


def _proj_kernel(o_ref, w_ref, x_ref, gate_ref, g2_ref, sh2_ref, sc2_ref, wr_ref,
                 xo_ref, h2_ref, aff_ref):
    y = jnp.dot(o_ref[...], w_ref[...], preferred_element_type=F32)
    x_new = x_ref[...] + gate_ref[...] * y
    _epilogue(x_new, g2_ref[...], sh2_ref[...], sc2_ref[...], wr_ref[...], xo_ref, h2_ref, aff_ref)


def _epilogue_specs(d, tm, blocks_per_mod):
    mod_spec = pl.BlockSpec((None, 1, d), lambda i: (i // blocks_per_mod, 0, 0))
    in_specs = [pl.BlockSpec((1, d), lambda i: (0, 0)), mod_spec, mod_spec,
                pl.BlockSpec((N_EXPERTS, d), lambda i: (0, 0))]
    out_specs = [pl.BlockSpec((tm, d), lambda i: (i, 0)),
                 pl.BlockSpec((tm, d), lambda i: (i, 0)),
                 pl.BlockSpec((tm // LANES, N_EXPERTS, LANES), lambda i: (i, 0, 0))]
    return in_specs, out_specs


def _epilogue_out_shape(n, d):
    return [jax.ShapeDtypeStruct((n, d), F32),
            jax.ShapeDtypeStruct((n, d), F32),
            jax.ShapeDtypeStruct((n // LANES, N_EXPERTS, LANES), F32)]


def _proj_mixer(o, w_bf16, x2d, gate1, g2, sh2, sc2, wr_t, tokens_per_mod):
    n, d = x2d.shape
    tm = ROW_TILE
    kdim = o.shape[1]
    bpm = tokens_per_mod // tm
    ep_in, ep_out = _epilogue_specs(d, tm, bpm)
    return pl.pallas_call(
        _proj_kernel,
        grid=(n // tm,),
        in_specs=[pl.BlockSpec((tm, kdim), lambda i: (i, 0)),
                  pl.BlockSpec((kdim, d), lambda i: (0, 0)),
                  pl.BlockSpec((tm, d), lambda i: (i, 0)),
                  pl.BlockSpec((None, 1, d), lambda i: (i // bpm, 0, 0))] + ep_in,
        out_specs=ep_out,
        out_shape=_epilogue_out_shape(n, d),
        compiler_params=_cparams(("parallel",)),
        name="proj_mixer",
    )(o, w_bf16, x2d, gate1, g2.reshape(1, d), sh2, sc2, wr_t)


def _pool_kernel(x_ref, xprev_ref, xnext_ref, g1_ref, sh1_ref, sc1_ref, pw_ref, pb_ref, ps_ref,
                 gate_ref, g2_ref, sh2_ref, sc2_ref, wr_ref, xo_ref, h2_ref, aff_ref, *, seq_len):
    tm, d = x_ref.shape
    halo = POOL_HALO
    i = pl.program_id(0)
    pos0 = (i * tm) % seq_len
    g1, sh1, sc1 = g1_ref[...], sh1_ref[...], sc1_ref[...]
    x = x_ref[...]
    h = _adaln_rows(x, g1, sh1, sc1)
    hprev = _adaln_rows(xprev_ref[...], g1, sh1, sc1)
    hnext = _adaln_rows(xnext_ref[...], g1, sh1, sc1)
    hprev = jnp.where(pos0 > 0, hprev, 0.0)
    hnext = jnp.where(pos0 + tm < seq_len, hnext, 0.0)
    hext = jnp.concatenate([hprev, h, hnext], axis=0)
    rows = tm + 2 * halo
    t = pos0 + lax.broadcasted_iota(jnp.int32, (tm, 1), 0)
    outs = []
    for gi, win in enumerate(POOL_WINDOWS):
        a = hext[:, gi * POOL_GROUP:(gi + 1) * POOL_GROUP]
        p = pltpu.roll(a, 1, axis=0) + a
        step = 1
        while 2 * step < win:
            p = pltpu.roll(p, step, axis=0) + pltpu.roll(p, rows - step, axis=0)
            step *= 2
        half = win // 2
        cnt = (jnp.minimum(t + half, seq_len) - jnp.maximum(t - half, 0)).astype(F32)
        pooled = p[halo:halo + tm, :] / cnt - a[halo:halo + tm, :]
        y = jnp.dot(pooled.astype(BF16), pw_ref[gi].astype(BF16), preferred_element_type=F32)
        outs.append(y + pb_ref[gi:gi + 1, :])
    mix = jnp.concatenate(outs, axis=1) * ps_ref[...]
    x_new = x + gate_ref[...] * mix
    _epilogue(x_new, g2_ref[...], sh2_ref[...], sc2_ref[...], wr_ref[...], xo_ref, h2_ref, aff_ref)


def _pool_mixer(x2d, g1, sh1, sc1, pool_w, pool_b, pool_scale, gate1, g2, sh2, sc2, wr_t,
                tokens_per_mod, seq_len):
    n, d = x2d.shape
    tm = min(ROW_TILE, seq_len)
    bpm = tokens_per_mod // tm
    hb = tm // POOL_HALO
    n_hblocks = n // POOL_HALO
    mod_spec = pl.BlockSpec((None, 1, d), lambda i: (i // bpm, 0, 0))
    ep_in, ep_out = _epilogue_specs(d, tm, bpm)
    return pl.pallas_call(
        functools.partial(_pool_kernel, seq_len=seq_len),
        grid=(n // tm,),
        in_specs=[pl.BlockSpec((tm, d), lambda i: (i, 0)),
                  pl.BlockSpec((POOL_HALO, d), lambda i: (jnp.maximum(i * hb - 1, 0), 0)),
                  pl.BlockSpec((POOL_HALO, d),
                               lambda i: (jnp.minimum((i + 1) * hb, n_hblocks - 1), 0)),
                  pl.BlockSpec((1, d), lambda i: (0, 0)), mod_spec, mod_spec,
                  pl.BlockSpec((N_POOL_GROUPS, POOL_GROUP, POOL_GROUP), lambda i: (0, 0, 0)),
                  pl.BlockSpec((N_POOL_GROUPS, POOL_GROUP), lambda i: (0, 0)),
                  pl.BlockSpec((1, d), lambda i: (0, 0)),
                  mod_spec] + ep_in,
        out_specs=ep_out,
        out_shape=_epilogue_out_shape(n, d),
        compiler_params=_cparams(("parallel",)),
        name="pool_mixer",
    )(x2d, x2d, x2d, g1.reshape(1, d), sh1, sc1, pool_w, pool_b, pool_scale.reshape(1, d),
      gate1, g2.reshape(1, d), sh2, sc2, wr_t)


def _ffn_kernel(idx_ref, h2a_hbm, h2b_hbm, wg_ref, wu_ref, wd_ref, g_ref, ye_ref, xbuf, sem):
    n_e, n_s = pl.num_programs(0), pl.num_programs(1)
    ei, si = pl.program_id(0), pl.program_id(1)
    n_blocks, sub = xbuf.shape[1], xbuf.shape[2]
    cap = n_blocks * sub
    rows = min(FFN_ROWS, cap)
    step = ei * n_s + si
    slot = step % 2

    def issue(e_t, s_t, slot_t):
        base = (s_t * n_e + e_t) * cap
        for set_id, src in enumerate((h2a_hbm, h2b_hbm)):
            @pl.when(s_t == set_id)
            def _(src=src):
                def body(kb, carry):
                    k0 = kb * sub
                    for u in range(sub):
                        pltpu.async_copy(src.at[pl.ds(idx_ref[base + k0 + u], 1)],
                                         xbuf.at[slot_t, kb, pl.ds(u, 1)], sem.at[slot_t],
                                         priority=u % 2)
                    return carry

                lax.fori_loop(0, n_blocks, body, 0)

    @pl.when(step == 0)
    def _():
        issue(ei, si, slot)

    @pl.when(step + 1 < n_e * n_s)
    def _():
        nxt = step + 1
        issue(nxt // n_s, nxt % n_s, 1 - slot)

    def wait_body(kb, carry):
        pltpu.make_async_copy(h2a_hbm.at[pl.ds(0, sub)], xbuf.at[slot, 0], sem.at[slot]).wait()
        return carry

    lax.fori_loop(0, n_blocks, wait_body, 0)

    def body(i, carry):
        r0 = pl.multiple_of(i * rows, rows)
        b0 = pl.multiple_of(i * (rows // sub), rows // sub)
        x = xbuf[slot, pl.ds(b0, rows // sub)].reshape(rows, xbuf.shape[3]).astype(BF16)
        a = jnp.dot(x, wg_ref[...], preferred_element_type=F32)
        b = jnp.dot(x, wu_ref[...], preferred_element_type=F32)
        hid = (a * (1.0 / (1.0 + jnp.exp(-a))) * b).astype(BF16)
        y = jnp.dot(hid, wd_ref[...], preferred_element_type=F32)
        gate = g_ref[pl.ds(r0, rows), :]
        for j in range(y.shape[1] // LANES):
            ye_ref[pl.ds(r0, rows), j * LANES:(j + 1) * LANES] = (
                y[:, j * LANES:(j + 1) * LANES] * gate).astype(ye_ref.dtype)
        return carry

    lax.fori_loop(0, cap // rows, body, 0)


def _expert_ffn(idx_flat, h2a, h2b, wg, wu, wd, gates, layer):
    s, e, c, _ = gates.shape
    assert s == 2
    d, f = wg.shape[2], wg.shape[3]
    return pl.pallas_call(
        _ffn_kernel,
        grid_spec=pltpu.PrefetchScalarGridSpec(
            num_scalar_prefetch=1,
            grid=(e, s),
            in_specs=[pl.BlockSpec(memory_space=pl.ANY),
                      pl.BlockSpec(memory_space=pl.ANY),
                      pl.BlockSpec((None, None, d, f), lambda ei, si, idx: (layer, ei, 0, 0)),
                      pl.BlockSpec((None, None, d, f), lambda ei, si, idx: (layer, ei, 0, 0)),
                      pl.BlockSpec((None, None, f, d), lambda ei, si, idx: (layer, ei, 0, 0)),
                      pl.BlockSpec((None, None, c, LANES), lambda ei, si, idx: (si, ei, 0, 0))],
            out_specs=pl.BlockSpec((None, None, c, d), lambda ei, si, idx: (si, ei, 0, 0)),
            scratch_shapes=[pltpu.VMEM((2, c // SUBLANES, SUBLANES, d), F32),
                            pltpu.SemaphoreType.DMA((2,))]),
        out_shape=jax.ShapeDtypeStruct((s, e, c, d), BF16),
        compiler_params=pltpu.CompilerParams(dimension_semantics=("arbitrary", "arbitrary"),
                                             vmem_limit_bytes=VMEM_LIMIT,
                                             disable_bounds_checks=True),
        name="expert_ffn",
    )(idx_flat, h2a, h2b, wg, wu, wd, gates)


def _combine_kernel(clo_ref, ncnt_ref, ye_hbm, tok_hbm, x_ref, gate_ref, fg_ref, o_ref,
                    zbuf, tokbuf, acc_ref, sem, *, final, set_id):
    n_e = ye_hbm.shape[1]
    tm, d = x_ref.shape
    ch, grp = COMBINE_CH, COMBINE_GROUP
    i = pl.program_id(0)
    nt = pl.num_programs(0)
    slot = i % 2

    def chunk_copies(e, src_row, slot_t, dst_row):
        return (pltpu.make_async_copy(ye_hbm.at[set_id, e, pl.ds(src_row, ch), :],
                                      zbuf.at[slot_t, pl.ds(dst_row, ch), :], sem.at[slot_t]),
                pltpu.make_async_copy(tok_hbm.at[set_id, e, pl.ds(src_row, ch), :],
                                      tokbuf.at[slot_t, pl.ds(dst_row, ch), :], sem.at[slot_t]))

    def issue(tile, slot_t):
        pos = 0
        for e in range(n_e):
            lo = clo_ref[tile * n_e + e]

            def body(j, p, e=e, lo=lo):
                src = pl.multiple_of((lo + j) * ch, ch)
                dst = pl.multiple_of(p * ch, ch)
                for cp in chunk_copies(e, src, slot_t, dst):
                    cp.start()
                return p + 1

            pos = lax.fori_loop(0, ncnt_ref[tile * n_e + e], body, pos)

    def total_chunks(tile):
        tot = 0
        for e in range(n_e):
            tot = tot + ncnt_ref[tile * n_e + e]
        return tot

    @pl.when(i == 0)
    def _():
        zbuf[...] = jnp.zeros(zbuf.shape, zbuf.dtype)
        tokbuf[...] = jnp.full(tokbuf.shape, -1, jnp.int32)
        issue(i, slot)

    @pl.when(i + 1 < nt)
    def _():
        issue(i + 1, 1 - slot)

    total = total_chunks(i)

    def wait_body(j, carry):
        for cp in chunk_copies(0, 0, slot, 0):
            cp.wait()
        return carry

    lax.fori_loop(0, total, wait_body, 0)

    acc_ref[...] = jnp.zeros(acc_ref.shape, F32)
    tok_of_lane = i * tm + lax.broadcasted_iota(jnp.int32, (grp, tm), 1)
    row_in_group = lax.broadcasted_iota(jnp.int32, (grp, 1), 0)

    def group_body(gidx, carry):
        r0 = pl.multiple_of(gidx * grp, grp)
        tok = jnp.where(r0 + row_in_group < total * ch, tokbuf[slot, pl.ds(r0, grp), :], -1)
        tok = jnp.tile(tok, (1, tm // LANES))
        onehot_t = (tok == tok_of_lane).astype(BF16)
        acc_ref[...] += lax.dot_general(onehot_t, zbuf[slot, pl.ds(r0, grp), :],
                                        (((0,), (0,)), ((), ())), preferred_element_type=F32)
        return carry

    lax.fori_loop(0, (total * ch + grp - 1) // grp, group_body, 0)

    x = x_ref[...] + gate_ref[...] * acc_ref[...]
    if final:
        ms = jnp.mean(x * x, axis=-1, keepdims=True)
        x = x * lax.rsqrt(ms + EPS) * fg_ref[...]
    o_ref[...] = x


def _combine(ye, tok_col, toff, set_id, x2d, gate2, final_g, tokens_per_mod, final):
    n, d = x2d.shape
    _, e, c, _ = ye.shape
    tm, ch = COMBINE_TM, COMBINE_CH
    n_tiles = n // tm
    bpm = tokens_per_mod // tm
    start = toff[::tm // LANES, :, 0]
    first = jnp.concatenate([start, jnp.full((1, e), c, jnp.int32)], axis=0)
    clo = first[:-1] // ch
    chi = (first[1:] + ch - 1) // ch
    ncnt = jnp.maximum(chi - clo, 0)
    max_rows = e * (tm // ch + 2) * ch
    max_rows = ((max_rows + COMBINE_GROUP - 1) // COMBINE_GROUP) * COMBINE_GROUP
    return pl.pallas_call(
        functools.partial(_combine_kernel, final=final, set_id=set_id),
        grid_spec=pltpu.PrefetchScalarGridSpec(
            num_scalar_prefetch=2,
            grid=(n_tiles,),
            in_specs=[pl.BlockSpec(memory_space=pl.ANY),
                      pl.BlockSpec(memory_space=pl.ANY),
                      pl.BlockSpec((tm, d), lambda i, a, b: (i, 0)),
                      pl.BlockSpec((None, 1, d), lambda i, a, b: (i // bpm, 0, 0)),
                      pl.BlockSpec((1, d), lambda i, a, b: (0, 0))],
            out_specs=pl.BlockSpec((tm, d), lambda i, a, b: (i, 0)),
            scratch_shapes=[pltpu.VMEM((2, max_rows, d), BF16),
                            pltpu.VMEM((2, max_rows, LANES), jnp.int32),
                            pltpu.VMEM((tm, d), F32),
                            pltpu.SemaphoreType.DMA((2,))]),
        out_shape=jax.ShapeDtypeStruct((n, d), F32),
        compiler_params=_cparams(("arbitrary",)),
        name="combine_final" if final else "combine",
    )(clo.reshape(-1), ncnt.reshape(-1), ye, tok_col, x2d, gate2, final_g.reshape(1, d))


def _rope_tables(n_tokens):
    t = jnp.arange(n_tokens)
    row = (t // GRID_W).astype(F32)
    col = (t % GRID_W).astype(F32)
    half = HEAD_DIM // 2
    inv = ROPE_BASE ** (-jnp.arange(0, half, 2, dtype=F32) / half)
    ar, ac = row[:, None] * inv, col[:, None] * inv
    c64 = jnp.concatenate([jnp.cos(ar), jnp.cos(ar), jnp.cos(ac), jnp.cos(ac)], axis=-1)
    s64 = jnp.concatenate([-jnp.sin(ar), jnp.sin(ar), -jnp.sin(ac), jnp.sin(ac)], axis=-1)
    return jnp.tile(c64, (1, 2)), jnp.tile(s64, (1, 2))


def _tile_prefix(mask3, il_ref, to_ref, tot_ref):
    nt, ne, ln = mask3.shape
    m2 = mask3.astype(F32).astype(BF16).reshape(nt * ne, ln)
    r = lax.broadcasted_iota(jnp.int32, (ln, ln), 0)
    c = lax.broadcasted_iota(jnp.int32, (ln, ln), 1)
    upper = (r <= c).astype(F32).astype(BF16)
    ones = jnp.ones((ln, ln), BF16)
    il_ref[...] = jnp.dot(m2, upper, preferred_element_type=F32).reshape(nt, ne, ln)
    tot_ref[...] = jnp.dot(m2, ones, preferred_element_type=F32).reshape(nt, ne, ln)
    run = jnp.zeros((ne, ln), F32)
    for tt in range(nt):
        to_ref[tt] = run
        run = run + tot_ref[tt]


def _split3_bf16(v):
    p1 = v.astype(BF16)
    r1 = v - p1.astype(F32)
    p2 = r1.astype(BF16)
    return p1, p2, (r1 - p2.astype(F32)).astype(BF16)


def _route_kernel(aff_ref, idx_ref, gcol_ref, tokcol_ref, toff_ref, il_ref, to_ref, tot_ref, *, cap):
    nt, ne, ln = aff_ref.shape
    a3 = aff_ref[...]

    def count(mask3):
        per_lane = jnp.sum(mask3.astype(jnp.int32), axis=0)
        return jnp.sum(per_lane, axis=1, keepdims=True)

    def as_float(thr_bits):
        return pltpu.bitcast(jnp.broadcast_to(thr_bits, (ne, ln)), F32)[:, 0:1]

    def search(i, thr_bits):
        cand = thr_bits | lax.shift_left(jnp.int32(1), 30 - i)
        return jnp.where(count(a3 >= as_float(cand)[None]) >= cap, cand, thr_bits)

    thr = as_float(lax.fori_loop(0, 31, search, jnp.zeros((ne, 1), jnp.int32)))
    gt = a3 > thr[None]
    eq = a3 == thr[None]
    need = (cap - count(gt)).astype(F32)
    _tile_prefix(eq, il_ref, to_ref, tot_ref)
    eq_before = to_ref[...] + il_ref[...] - eq.astype(F32)
    sel = gt | (eq & (eq_before < need[None]))
    _tile_prefix(sel, il_ref, to_ref, tot_ref)
    toff_ref[...] = to_ref[...].astype(jnp.int32)

    reps = cap // ln
    ones = jnp.ones((ln, ln), BF16)

    def to_column(vals, bf16_exact=False):
        out = None
        for piece in ((vals.astype(BF16),) if bf16_exact else _split3_bf16(vals)):
            t = lax.dot_general(piece, ones, (((0,), (0,)), ((), ())), preferred_element_type=F32)
            out = t if out is None else out + t
        return out

    slot = lax.broadcasted_iota(jnp.int32, (nt, cap), 1).astype(F32)
    tile_id = lax.broadcasted_iota(jnp.int32, (nt, cap), 0).astype(F32)
    in_tile = lax.broadcasted_iota(jnp.int32, (ln, cap), 0).astype(F32)
    for e in range(ne):
        il_e = il_ref[:, e, :]
        to_e = jnp.tile(to_ref[:, e, :], (1, reps))
        tot_e = jnp.tile(tot_ref[:, e, :], (1, reps))
        owns = (to_e <= slot) & (slot < to_e + tot_e)
        rank = jnp.sum(jnp.where(owns, slot - to_e, 0.0), axis=0, keepdims=True)
        tile_of = jnp.sum(jnp.where(owns, tile_id, 0.0), axis=0, keepdims=True)
        owns_b = owns.astype(F32).astype(BF16)
        contract0 = (((0,), (0,)), ((), ()))
        il_g = lax.dot_general(il_e.astype(BF16), owns_b, contract0,
                               preferred_element_type=F32)
        pos = jnp.sum((il_g <= rank).astype(F32), axis=0, keepdims=True)
        idx_ref[e:e + 1, :] = (tile_of * ln + pos).astype(jnp.int32)
        p1, p2, p3 = _split3_bf16(aff_ref[:, e, :])
        a_g = (lax.dot_general(p1, owns_b, contract0, preferred_element_type=F32)
               + lax.dot_general(p2, owns_b, contract0, preferred_element_type=F32)
               + lax.dot_general(p3, owns_b, contract0, preferred_element_type=F32))
        hit = in_tile == pos
        gcol_ref[e] = to_column(jnp.where(hit, a_g, 0.0))
        tok_hi = to_column(jnp.where(hit, tile_of, 0.0), bf16_exact=True)
        tok_lo = to_column(jnp.where(hit, pos, 0.0), bf16_exact=True)
        tokcol_ref[e] = (tok_hi * ln + tok_lo).astype(jnp.int32)


def _route(aff_all, cap):
    s, nt, ne, ln = aff_all.shape
    assert nt <= 256 and cap % ln == 0
    return pl.pallas_call(
        functools.partial(_route_kernel, cap=cap),
        grid=(s,),
        in_specs=[pl.BlockSpec((None, nt, ne, ln), lambda i: (i, 0, 0, 0))],
        out_specs=[pl.BlockSpec((None, ne, cap), lambda i: (i, 0, 0)),
                   pl.BlockSpec((None, ne, cap, ln), lambda i: (i, 0, 0, 0)),
                   pl.BlockSpec((None, ne, cap, ln), lambda i: (i, 0, 0, 0)),
                   pl.BlockSpec((None, nt, ne, ln), lambda i: (i, 0, 0, 0))],
        out_shape=[jax.ShapeDtypeStruct((s, ne, cap), jnp.int32),
                   jax.ShapeDtypeStruct((s, ne, cap, ln), F32),
                   jax.ShapeDtypeStruct((s, ne, cap, ln), jnp.int32),
                   jax.ShapeDtypeStruct((s, nt, ne, ln), jnp.int32)],
        scratch_shapes=[pltpu.VMEM((nt, ne, ln), F32)] * 3,
        compiler_params=_cparams(("parallel",)),
        name="route",
    )(aff_all)


def _route_and_ffn(h2_list, aff_list, w_gate, w_up, w_down, layer):
    n = h2_list[0].shape[0]
    assert all(h.shape[0] == n for h in h2_list)
    cap = EC_CAPACITY * n // N_EXPERTS
    idx_all, gcol, tokcol, toff = _route(jnp.stack(aff_list), cap)
    ye = _expert_ffn(idx_all.reshape(-1), h2_list[0], h2_list[1], w_gate, w_up, w_down, gcol, layer)
    return ye, tokcol, toff


def kernel(x_prompt, x_sample, cache_k, cache_v, c, c_ctx, w_mod, b_mod, norm1_g, norm2_g,
           attn_wqkv, attn_wo, lambda_q1, lambda_k1, lambda_q2, lambda_k2, subln_g,
           pool_w, pool_b, pool_scale, router_w, expert_w_gate, expert_w_up, expert_w_down,
           final_g):
    bp, n_ctx, d = x_prompt.shape
    bs, n_lat, _ = x_sample.shape
    past = cache_k.shape[2]
    n_p, n_s = bp * n_ctx, bs * n_lat

    rows = 8 * ((1 + bs + 7) // 8)
    cvecs = jnp.zeros((rows, d), F32).at[0].set(c_ctx).at[1:1 + bs].set(c)
    mod = _modulation_all(cvecs, w_mod, b_mod)
    rope_tabs = _rope_tables(n_lat)
    cache_k4 = cache_k.reshape(bs, cache_k.shape[1], past, ATTN_WIDTH)
    cache_v4 = cache_v.reshape(bs, cache_v.shape[1], past, ATTN_WIDTH)
    w_gate_b = expert_w_gate.astype(BF16)
    w_up_b = expert_w_up.astype(BF16)
    w_down_b = expert_w_down.astype(BF16)

    xp = x_prompt.reshape(n_p, d)
    xs = x_sample.reshape(n_s, d)
    new_k, new_v = [], []
    for l in range(DEPTH):
        mp_ = mod[l, 0:1].reshape(1, 6, 1, d)
        ms_ = mod[l, 1:1 + bs].reshape(bs, 6, 1, d)
        sp1, cp1, gp1, sp2, cp2, gp2 = [mp_[:, i] for i in range(6)]
        ss1, cs1, gs1, ss2, cs2, gs2 = [ms_[:, i] for i in range(6)]
        wr_t = router_w[l].T.astype(BF16)
        if l % 2 == 0:
            a = l // 2
            lam_init = 0.8 - 0.6 * math.exp(-0.3 * l)
            lamv = jnp.stack([lambda_q1[a], lambda_k1[a], lambda_q2[a], lambda_k2[a]])
            wqkv = attn_wqkv[a].astype(BF16)
            wo = attn_wo[a].astype(BF16)
            qp, kp, vp = _qkv_project(xp, norm1_g[l], sp1, cp1, wqkv, None, n_p, F32)
            new_k.append(kp.reshape(bp, n_ctx, N_HEADS, 2 * HEAD_DIM))
            new_v.append(vp.reshape(bp, n_ctx, N_HEADS, V_DIM))
            op = _diff_attention(qp, kp, vp, lamv, subln_g[a], lam_init, bp)
            xp, h2p, affp = _proj_mixer(op, wo, xp, gp1, norm2_g[l], sp2, cp2, wr_t, n_p)
            qs, ks_, vs = _qkv_project(xs, norm1_g[l], ss1, cs1, wqkv, rope_tabs, n_lat, BF16)
            os_ = _diff_attention(qs, ks_, vs, lamv, subln_g[a], lam_init, bs,
                                  cache=(cache_k4, cache_v4, a))
            xs, h2s, affs = _proj_mixer(os_, wo, xs, gs1, norm2_g[l], ss2, cs2, wr_t, n_lat)
        else:
            p = l // 2
            xp, h2p, affp = _pool_mixer(xp, norm1_g[l], sp1, cp1, pool_w[p], pool_b[p],
                                        pool_scale[p], gp1, norm2_g[l], sp2, cp2, wr_t,
                                        n_p, n_ctx)
            xs, h2s, affs = _pool_mixer(xs, norm1_g[l], ss1, cs1, pool_w[p], pool_b[p],
                                        pool_scale[p], gs1, norm2_g[l], ss2, cs2, wr_t,
                                        n_lat, n_lat)
        ye, tokcol, toff = _route_and_ffn([h2p, h2s], [affp, affs], w_gate_b, w_up_b, w_down_b, l)
        last = l == DEPTH - 1
        xp = _combine(ye, tokcol, toff[0], 0, xp, gp2, final_g, n_p, last)
        xs = _combine(ye, tokcol, toff[1], 1, xs, gs2, final_g, n_lat, last)
    return (xp.reshape(bp, n_ctx, d), xs.reshape(bs, n_lat, d),
            jnp.stack(new_k, axis=1), jnp.stack(new_v, axis=1))
```

```python
import math
import functools
import jax
import jax.numpy as jnp
from jax import lax
from jax.experimental import pallas as pl
from jax.experimental.pallas import tpu as pltpu

D_MODEL = 1024
DEPTH = 4
GRID_W = 64
N_HEADS = 8
HEAD_DIM = 64
V_DIM = 2 * HEAD_DIM
ATTN_WIDTH = N_HEADS * V_DIM
ROPE_BASE = 10000.0
N_POOL_GROUPS = 4
POOL_GROUP = D_MODEL // N_POOL_GROUPS
POOL_WINDOWS = (2, 4, 8, 16)
POOL_HALO = 8
N_EXPERTS = 16
EC_CAPACITY = 2
EPS = 1e-6

F32 = jnp.float32
BF16 = jnp.bfloat16
LANES = 128
LOG2E = 1.4426950408889634
VMEM_LIMIT = 56 * 1024 * 1024
ROW_TILE = 512
ATTN_TQ = 512
ATTN_TK = 4096
ATTN_HEADS_PER_STEP = 2
ATTN_SHORT_KEYS = 512
FFN_ROWS = 512
SUBLANES = 8
COMBINE_TM = 256
COMBINE_CH = 32
COMBINE_GROUP = 256


def _cparams(sem):
    return pltpu.CompilerParams(dimension_semantics=sem, vmem_limit_bytes=VMEM_LIMIT)


def _mod_kernel(c_ref, w_ref, b_ref, o_ref):
    c = c_ref[...]
    s = (c * (1.0 / (1.0 + jnp.exp(-c)))).astype(BF16)
    o_ref[...] = jnp.dot(s, w_ref[...].astype(BF16), preferred_element_type=F32) + b_ref[...]


def _modulation_all(cvecs, w_mod, b_mod):
    r, d = cvecs.shape
    depth, _, n6 = w_mod.shape
    tn = 1024
    return pl.pallas_call(
        _mod_kernel,
        grid=(depth, n6 // tn),
        in_specs=[pl.BlockSpec((r, d), lambda l, j: (0, 0)),
                  pl.BlockSpec((None, d, tn), lambda l, j: (l, 0, j)),
                  pl.BlockSpec((None, 1, tn), lambda l, j: (l, 0, j))],
        out_specs=pl.BlockSpec((None, r, tn), lambda l, j: (l, 0, j)),
        out_shape=jax.ShapeDtypeStruct((depth, r, n6), F32),
        compiler_params=_cparams(("parallel", "parallel")),
        name="modulation",
    )(cvecs, w_mod, b_mod.reshape(depth, 1, n6))


def _adaln_rows(x, g, shift, scale):
    ms = jnp.mean(x * x, axis=-1, keepdims=True)
    return (x * lax.rsqrt(ms + EPS)) * g * (1.0 + scale) + shift


def _swap16(x):
    lane = lax.broadcasted_iota(jnp.int32, x.shape, 1)
    up = pltpu.roll(x, x.shape[1] - 16, axis=1)
    dn = pltpu.roll(x, 16, axis=1)
    return jnp.where((lane % 32) < 16, up, dn)


def _qkv_kernel(*refs, rope):
    if rope:
        x_ref, g_ref, sh_ref, sc_ref, w_ref, cos_ref, sin_ref, q_ref, k_ref, v_ref = refs
    else:
        x_ref, g_ref, sh_ref, sc_ref, w_ref, q_ref, k_ref, v_ref = refs
    h = _adaln_rows(x_ref[...], g_ref[...], sh_ref[...], sc_ref[...]).astype(BF16)
    width = q_ref.shape[1]
    qscale = (HEAD_DIM ** -0.5) * LOG2E
    for part, o_ref in enumerate((q_ref, k_ref, v_ref)):
        r = jnp.dot(h, w_ref[:, part * width:(part + 1) * width], preferred_element_type=F32)
        if part < 2 and rope:
            c = cos_ref[...]
            s = sin_ref[...]
            for hd in range(width // LANES):
                blk = r[:, hd * LANES:(hd + 1) * LANES]
                blk = blk * c + _swap16(blk) * s
                if part == 0:
                    blk = blk * qscale
                o_ref[:, hd * LANES:(hd + 1) * LANES] = blk.astype(o_ref.dtype)
        else:
            if part == 0:
                r = r * qscale
            o_ref[...] = r.astype(o_ref.dtype)


def _qkv_project(x2d, g, shift, scale, w_bf16, rope_tabs, tokens_per_mod, kv_dtype):
    n, d = x2d.shape
    tm = ROW_TILE
    width = w_bf16.shape[1] // 3
    rope = rope_tabs is not None
    blocks_per_mod = tokens_per_mod // tm
    in_specs = [pl.BlockSpec((tm, d), lambda i: (i, 0)),
                pl.BlockSpec((1, d), lambda i: (0, 0)),
                pl.BlockSpec((None, 1, d), lambda i: (i // blocks_per_mod, 0, 0)),
                pl.BlockSpec((None, 1, d), lambda i: (i // blocks_per_mod, 0, 0)),
                pl.BlockSpec((d, 3 * width), lambda i: (0, 0))]
    args = [x2d, g.reshape(1, d), shift, scale, w_bf16]
    if rope:
        cos_t, sin_t = rope_tabs
        seq_blocks = cos_t.shape[0] // tm
        in_specs += [pl.BlockSpec((tm, LANES), lambda i: (i % seq_blocks, 0)),
                     pl.BlockSpec((tm, LANES), lambda i: (i % seq_blocks, 0))]
        args += [cos_t, sin_t]
    out_spec = pl.BlockSpec((tm, width), lambda i: (i, 0))
    return pl.pallas_call(
        functools.partial(_qkv_kernel, rope=rope),
        grid=(n // tm,),
        in_specs=in_specs,
        out_specs=[out_spec, out_spec, out_spec],
        out_shape=[jax.ShapeDtypeStruct((n, width), BF16),
                   jax.ShapeDtypeStruct((n, width), kv_dtype),
                   jax.ShapeDtypeStruct((n, width), kv_dtype)],
        compiler_params=_cparams(("parallel",)),
        name="adaln_qkv_rope" if rope else "adaln_qkv",
    )(*args)


def _attn_kernel(*refs, tk, lam_init, hp, cached):
    if cached:
        (lamv_ref, q_ref, k_ref, v_ref, kc_ref, vc_ref, g_ref, o_ref,
         qs_ref, m_ref, l_ref, acc_ref) = refs
    else:
        lamv_ref, q_ref, k_ref, v_ref, g_ref, o_ref, qs_ref, m_ref, l_ref, acc_ref = refs
    tq = q_ref.shape[0]
    nk = k_ref.shape[0]
    for h in range(hp):
        q = q_ref[:, h * LANES:(h + 1) * LANES]
        lane = lax.broadcasted_iota(jnp.int32, q.shape, 1)
        zero = jnp.zeros_like(q)
        qs_ref[h, 0:tq, :] = jnp.where(lane < HEAD_DIM, q, zero)
        qs_ref[h, tq:2 * tq, :] = jnp.where(lane >= HEAD_DIM, q, zero)
    m_ref[...] = jnp.full(m_ref.shape, -jnp.inf, F32)
    l_ref[...] = jnp.zeros(l_ref.shape, F32)
    acc_ref[...] = jnp.zeros(acc_ref.shape, F32)

    def online_softmax_step(h, kc, vc):
        s = lax.dot_general(qs_ref[h], kc, (((1,), (1,)), ((), ())),
                            preferred_element_type=F32)
        m_old = m_ref[h]
        m_new = jnp.maximum(m_old, jnp.max(s, axis=1)[:, None])
        alpha = jnp.exp2(m_old - m_new)
        p = jnp.exp2(s - jnp.tile(m_new, (1, s.shape[1] // LANES)))
        l_ref[h] = alpha * l_ref[h] + jnp.sum(p, axis=1)[:, None]
        acc_ref[h] = alpha * acc_ref[h] + jnp.dot(p.astype(BF16), vc, preferred_element_type=F32)
        m_ref[h] = m_new

    def body(j, carry):
        off = pl.multiple_of(j * tk, tk)
        for h in range(hp):
            online_softmax_step(h, k_ref[pl.ds(off, tk), h * LANES:(h + 1) * LANES].astype(BF16),
                                v_ref[pl.ds(off, tk), h * LANES:(h + 1) * LANES].astype(BF16))
        return carry

    lax.fori_loop(0, nk // tk, body, 0)
    if cached:
        for h in range(hp):
            online_softmax_step(h, kc_ref[:, h * LANES:(h + 1) * LANES].astype(BF16),
                                vc_ref[:, h * LANES:(h + 1) * LANES].astype(BF16))

    lv = lamv_ref[...]
    lam = (jnp.exp(jnp.sum(lv[0:1, :] * lv[1:2, :], axis=-1, keepdims=True))
           - jnp.exp(jnp.sum(lv[2:3, :] * lv[3:4, :], axis=-1, keepdims=True)) + lam_init)
    for h in range(hp):
        o = acc_ref[h] / l_ref[h]
        d = o[0:tq, :] - lam * o[tq:2 * tq, :]
        ms = jnp.mean(d * d, axis=-1, keepdims=True)
        o_ref[:, h * LANES:(h + 1) * LANES] = (
            (d * lax.rsqrt(ms + EPS)) * g_ref[...] * (1.0 - lam_init)).astype(o_ref.dtype)


def _diff_attention(q, k, v, lamv, subln_g, lam_init, batch, cache=None):
    nq = q.shape[0] // batch
    nk = k.shape[0] // batch
    tq = min(ATTN_TQ, nq)
    tk = min(ATTN_TK, nk)
    assert nq % tq == 0 and nk % tk == 0 and tk % LANES == 0
    hp = N_HEADS if nk <= ATTN_SHORT_KEYS else ATTN_HEADS_PER_STEP
    w = hp * LANES
    q3 = q.reshape(batch, nq, ATTN_WIDTH)
    k3 = k.reshape(batch, nk, ATTN_WIDTH)
    v3 = v.reshape(batch, nk, ATTN_WIDTH)
    kv_spec = pl.BlockSpec((None, nk, w), lambda b, h, i: (b, 0, h))
    in_specs = [pl.BlockSpec((4, HEAD_DIM), lambda b, h, i: (0, 0)),
                pl.BlockSpec((None, tq, w), lambda b, h, i: (b, i, h)), kv_spec, kv_spec]
    args = [lamv, q3, k3, v3]
    if cache is not None:
        ck, cv, layer = cache
        past = ck.shape[2]
        assert past % LANES == 0
        c_spec = pl.BlockSpec((None, None, past, w), lambda b, h, i: (b, layer, 0, h))
        in_specs += [c_spec, c_spec]
        args += [ck, cv]
    in_specs.append(pl.BlockSpec((1, V_DIM), lambda b, h, i: (0, 0)))
    args.append(subln_g.reshape(1, V_DIM))
    out = pl.pallas_call(
        functools.partial(_attn_kernel, tk=tk, lam_init=lam_init, hp=hp, cached=cache is not None),
        grid=(batch, N_HEADS // hp, nq // tq),
        in_specs=in_specs,
        out_specs=pl.BlockSpec((None, tq, w), lambda b, h, i: (b, i, h)),
        out_shape=jax.ShapeDtypeStruct((batch, nq, ATTN_WIDTH), BF16),
        scratch_shapes=[pltpu.VMEM((hp, 2 * tq, V_DIM), BF16),
                        pltpu.VMEM((hp, 2 * tq, LANES), F32),
                        pltpu.VMEM((hp, 2 * tq, LANES), F32),
                        pltpu.VMEM((hp, 2 * tq, V_DIM), F32)],
        compiler_params=_cparams(("parallel", "parallel", "parallel")),
        name="diff_attention",
    )(*args)
    return out.reshape(batch * nq, ATTN_WIDTH)


def _epilogue(x_new, g2, sh2, sc2, wr_t, xo_ref, h2_ref, aff_ref):
    xo_ref[...] = x_new
    h2 = _adaln_rows(x_new, g2, sh2, sc2)
    h2b = h2.astype(BF16)
    h2_ref[...] = h2
    logits = lax.dot_general(wr_t, h2b, (((1,), (1,)), ((), ())),
                             preferred_element_type=F32)
    mx = jnp.max(logits, axis=0, keepdims=True)
    ex = jnp.exp(logits - mx)
    aff = ex / jnp.sum(ex, axis=0, keepdims=True)
    for r in range(aff_ref.shape[0]):
        aff_ref[r] = aff[:, r * LANES:(r + 1) * LANES]


def _proj_kernel(o_ref, w_ref, x_ref, gate_ref, g2_ref, sh2_ref, sc2_ref, wr_ref,
                 xo_ref, h2_ref, aff_ref):
    y = jnp.dot(o_ref[...], w_ref[...], preferred_element_type=F32)
    x_new = x_ref[...] + gate_ref[...] * y
    _epilogue(x_new, g2_ref[...], sh2_ref[...], sc2_ref[...], wr_ref[...], xo_ref, h2_ref, aff_ref)


def _epilogue_specs(d, tm, blocks_per_mod):
    mod_spec = pl.BlockSpec((None, 1, d), lambda i: (i // blocks_per_mod, 0, 0))
    in_specs = [pl.BlockSpec((1, d), lambda i: (0, 0)), mod_spec, mod_spec,
                pl.BlockSpec((N_EXPERTS, d), lambda i: (0, 0))]
    out_specs = [pl.BlockSpec((tm, d), lambda i: (i, 0)),
                 pl.BlockSpec((tm, d), lambda i: (i, 0)),
                 pl.BlockSpec((tm // LANES, N_EXPERTS, LANES), lambda i: (i, 0, 0))]
    return in_specs, out_specs


def _epilogue_out_shape(n, d):
    return [jax.ShapeDtypeStruct((n, d), F32),
            jax.ShapeDtypeStruct((n, d), F32),
            jax.ShapeDtypeStruct((n // LANES, N_EXPERTS, LANES), F32)]


def _proj_mixer(o, w_bf16, x2d, gate1, g2, sh2, sc2, wr_t, tokens_per_mod):
    n, d = x2d.shape
    tm = ROW_TILE
    kdim = o.shape[1]
    bpm = tokens_per_mod // tm
    ep_in, ep_out = _epilogue_specs(d, tm, bpm)
    return pl.pallas_call(
        _proj_kernel,
        grid=(n // tm,),
        in_specs=[pl.BlockSpec((tm, kdim), lambda i: (i, 0)),
                  pl.BlockSpec((kdim, d), lambda i: (0, 0)),
                  pl.BlockSpec((tm, d), lambda i: (i, 0)),
                  pl.BlockSpec((None, 1, d), lambda i: (i // bpm, 0, 0))] + ep_in,
        out_specs=ep_out,
        out_shape=_epilogue_out_shape(n, d),
        compiler_params=_cparams(("parallel",)),
        name="proj_mixer",
    )(o, w_bf16, x2d, gate1, g2.reshape(1, d), sh2, sc2, wr_t)


def _pool_kernel(x_ref, xprev_ref, xnext_ref, g1_ref, sh1_ref, sc1_ref, pw_ref, pb_ref, ps_ref,
                 gate_ref, g2_ref, sh2_ref, sc2_ref, wr_ref, xo_ref, h2_ref, aff_ref, *, seq_len):
    tm, d = x_ref.shape
    halo = POOL_HALO
    i = pl.program_id(0)
    pos0 = (i * tm) % seq_len
    g1, sh1, sc1 = g1_ref[...], sh1_ref[...], sc1_ref[...]
    x = x_ref[...]
    h = _adaln_rows(x, g1, sh1, sc1)
    hprev = _adaln_rows(xprev_ref[...], g1, sh1, sc1)
    hnext = _adaln_rows(xnext_ref[...], g1, sh1, sc1)
    hprev = jnp.where(pos0 > 0, hprev, 0.0)
    hnext = jnp.where(pos0 + tm < seq_len, hnext, 0.0)
    hext = jnp.concatenate([hprev, h, hnext], axis=0)
    rows = tm + 2 * halo
    t = pos0 + lax.broadcasted_iota(jnp.int32, (tm, 1), 0)
    outs = []
    for gi, win in enumerate(POOL_WINDOWS):
        a = hext[:, gi * POOL_GROUP:(gi + 1) * POOL_GROUP]
        p = pltpu.roll(a, 1, axis=0) + a
        step = 1
        while 2 * step < win:
            p = pltpu.roll(p, step, axis=0) + pltpu.roll(p, rows - step, axis=0)
            step *= 2
        half = win // 2
        cnt = (jnp.minimum(t + half, seq_len) - jnp.maximum(t - half, 0)).astype(F32)
        pooled = p[halo:halo + tm, :] / cnt - a[halo:halo + tm, :]
        y = jnp.dot(pooled.astype(BF16), pw_ref[gi].astype(BF16), preferred_element_type=F32)
        outs.append(y + pb_ref[gi:gi + 1, :])
    mix = jnp.concatenate(outs, axis=1) * ps_ref[...]
    x_new = x + gate_ref[...] * mix
    _epilogue(x_new, g2_ref[...], sh2_ref[...], sc2_ref[...], wr_ref[...], xo_ref, h2_ref, aff_ref)


def _pool_mixer(x2d, g1, sh1, sc1, pool_w, pool_b, pool_scale, gate1, g2, sh2, sc2, wr_t,
                tokens_per_mod, seq_len):
    n, d = x2d.shape
    tm = min(ROW_TILE, seq_len)
    bpm = tokens_per_mod // tm
    hb = tm // POOL_HALO
    n_hblocks = n // POOL_HALO
    mod_spec = pl.BlockSpec((None, 1, d), lambda i: (i // bpm, 0, 0))
    ep_in, ep_out = _epilogue_specs(d, tm, bpm)
    return pl.pallas_call(
        functools.partial(_pool_kernel, seq_len=seq_len),
        grid=(n // tm,),
        in_specs=[pl.BlockSpec((tm, d), lambda i: (i, 0)),
                  pl.BlockSpec((POOL_HALO, d), lambda i: (jnp.maximum(i * hb - 1, 0), 0)),
                  pl.BlockSpec((POOL_HALO, d),
                               lambda i: (jnp.minimum((i + 1) * hb, n_hblocks - 1), 0)),
                  pl.BlockSpec((1, d), lambda i: (0, 0)), mod_spec, mod_spec,
                  pl.BlockSpec((N_POOL_GROUPS, POOL_GROUP, POOL_GROUP), lambda i: (0, 0, 0)),
                  pl.BlockSpec((N_POOL_GROUPS, POOL_GROUP), lambda i: (0, 0)),
                  pl.BlockSpec((1, d), lambda i: (0, 0)),
                  mod_spec] + ep_in,
        out_specs=ep_out,
        out_shape=_epilogue_out_shape(n, d),
        compiler_params=_cparams(("parallel",)),
        name="pool_mixer",
    )(x2d, x2d, x2d, g1.reshape(1, d), sh1, sc1, pool_w, pool_b, pool_scale.reshape(1, d),
      gate1, g2.reshape(1, d), sh2, sc2, wr_t)


def _ffn_kernel(idx_ref, h2a_hbm, h2b_hbm, wg_ref, wu_ref, wd_ref, g_ref, ye_ref, xbuf, sem):
    n_e, n_s = pl.num_programs(0), pl.num_programs(1)
    ei, si = pl.program_id(0), pl.program_id(1)
    n_blocks, sub = xbuf.shape[1], xbuf.shape[2]
    cap = n_blocks * sub
    rows = min(FFN_ROWS, cap)
    step = ei * n_s + si
    slot = step % 2

    def issue(e_t, s_t, slot_t):
        base = (s_t * n_e + e_t) * cap
        for set_id, src in enumerate((h2a_hbm, h2b_hbm)):
            @pl.when(s_t == set_id)
            def _(src=src):
                def body(kb, carry):
                    k0 = kb * sub
                    for u in range(sub):
                        pltpu.async_copy(src.at[pl.ds(idx_ref[base + k0 + u], 1)],
                                         xbuf.at[slot_t, kb, pl.ds(u, 1)], sem.at[slot_t],
                                         priority=u % 2)
                    return carry

                lax.fori_loop(0, n_blocks, body, 0)

    @pl.when(step == 0)
    def _():
        issue(ei, si, slot)

    @pl.when(step + 1 < n_e * n_s)
    def _():
        nxt = step + 1
        issue(nxt // n_s, nxt % n_s, 1 - slot)

    def wait_body(kb, carry):
        pltpu.make_async_copy(h2a_hbm.at[pl.ds(0, sub)], xbuf.at[slot, 0], sem.at[slot]).wait()
        return carry

    lax.fori_loop(0, n_blocks, wait_body, 0)

    def body(i, carry):
        r0 = pl.multiple_of(i * rows, rows)
        b0 = pl.multiple_of(i * (rows // sub), rows // sub)
        x = xbuf[slot, pl.ds(b0, rows // sub)].reshape(rows, xbuf.shape[3]).astype(BF16)
        a = jnp.dot(x, wg_ref[...], preferred_element_type=F32)
        b = jnp.dot(x, wu_ref[...], preferred_element_type=F32)
        hid = (a * (1.0 / (1.0 + jnp.exp(-a))) * b).astype(BF16)
        y = jnp.dot(hid, wd_ref[...], preferred_element_type=F32)
        gate = g_ref[pl.ds(r0, rows), :]
        for j in range(y.shape[1] // LANES):
            ye_ref[pl.ds(r0, rows), j * LANES:(j + 1) * LANES] = (
                y[:, j * LANES:(j + 1) * LANES] * gate).astype(ye_ref.dtype)
        return carry

    lax.fori_loop(0, cap // rows, body, 0)


def _expert_ffn(idx_flat, h2a, h2b, wg, wu, wd, gates, layer):
    s, e, c, _ = gates.shape
    assert s == 2
    d, f = wg.shape[2], wg.shape[3]
    return pl.pallas_call(
        _ffn_kernel,
        grid_spec=pltpu.PrefetchScalarGridSpec(
            num_scalar_prefetch=1,
            grid=(e, s),
            in_specs=[pl.BlockSpec(memory_space=pl.ANY),
                      pl.BlockSpec(memory_space=pl.ANY),
                      pl.BlockSpec((None, None, d, f), lambda ei, si, idx: (layer, ei, 0, 0)),
                      pl.BlockSpec((None, None, d, f), lambda ei, si, idx: (layer, ei, 0, 0)),
                      pl.BlockSpec((None, None, f, d), lambda ei, si, idx: (layer, ei, 0, 0)),
                      pl.BlockSpec((None, None, c, LANES), lambda ei, si, idx: (si, ei, 0, 0))],
            out_specs=pl.BlockSpec((None, None, c, d), lambda ei, si, idx: (si, ei, 0, 0)),
            scratch_shapes=[pltpu.VMEM((2, c // SUBLANES, SUBLANES, d), F32),
                            pltpu.SemaphoreType.DMA((2,))]),
        out_shape=jax.ShapeDtypeStruct((s, e, c, d), BF16),
        compiler_params=pltpu.CompilerParams(dimension_semantics=("arbitrary", "arbitrary"),
                                             vmem_limit_bytes=VMEM_LIMIT,
                                             disable_bounds_checks=True),
        name="expert_ffn",
    )(idx_flat, h2a, h2b, wg, wu, wd, gates)


def _combine_kernel(clo_ref, ncnt_ref, ye_hbm, tok_hbm, x_ref, gate_ref, fg_ref, o_ref,
                    zbuf, tokbuf, acc_ref, sem, *, final, set_id):
    n_e = ye_hbm.shape[1]
    tm, d = x_ref.shape
    ch, grp = COMBINE_CH, COMBINE_GROUP
    i = pl.program_id(0)
    nt = pl.num_programs(0)
    slot = i % 2

    def chunk_copies(e, src_row, slot_t, dst_row):
        return (pltpu.make_async_copy(ye_hbm.at[set_id, e, pl.ds(src_row, ch), :],
                                      zbuf.at[slot_t, pl.ds(dst_row, ch), :], sem.at[slot_t]),
                pltpu.make_async_copy(tok_hbm.at[set_id, e, pl.ds(src_row, ch), :],
                                      tokbuf.at[slot_t, pl.ds(dst_row, ch), :], sem.at[slot_t]))

    def issue(tile, slot_t):
        pos = 0
        for e in range(n_e):
            lo = clo_ref[tile * n_e + e]

            def body(j, p, e=e, lo=lo):
                src = pl.multiple_of((lo + j) * ch, ch)
                dst = pl.multiple_of(p * ch, ch)
                for cp in chunk_copies(e, src, slot_t, dst):
                    cp.start()
                return p + 1

            pos = lax.fori_loop(0, ncnt_ref[tile * n_e + e], body, pos)

    def total_chunks(tile):
        tot = 0
        for e in range(n_e):
            tot = tot + ncnt_ref[tile * n_e + e]
        return tot

    @pl.when(i == 0)
    def _():
        zbuf[...] = jnp.zeros(zbuf.shape, zbuf.dtype)
        tokbuf[...] = jnp.full(tokbuf.shape, -1, jnp.int32)
        issue(i, slot)

    @pl.when(i + 1 < nt)
    def _():
        issue(i + 1, 1 - slot)

    total = total_chunks(i)

    def wait_body(j, carry):
        for cp in chunk_copies(0, 0, slot, 0):
            cp.wait()
        return carry

    lax.fori_loop(0, total, wait_body, 0)

    acc_ref[...] = jnp.zeros(acc_ref.shape, F32)
    tok_of_lane = i * tm + lax.broadcasted_iota(jnp.int32, (grp, tm), 1)
    row_in_group = lax.broadcasted_iota(jnp.int32, (grp, 1), 0)

    def group_body(gidx, carry):
        r0 = pl.multiple_of(gidx * grp, grp)
        tok = jnp.where(r0 + row_in_group < total * ch, tokbuf[slot, pl.ds(r0, grp), :], -1)
        tok = jnp.tile(tok, (1, tm // LANES))
        onehot_t = (tok == tok_of_lane).astype(BF16)
        acc_ref[...] += lax.dot_general(onehot_t, zbuf[slot, pl.ds(r0, grp), :],
                                        (((0,), (0,)), ((), ())), preferred_element_type=F32)
        return carry

    lax.fori_loop(0, (total * ch + grp - 1) // grp, group_body, 0)

    x = x_ref[...] + gate_ref[...] * acc_ref[...]
    if final:
        ms = jnp.mean(x * x, axis=-1, keepdims=True)
        x = x * lax.rsqrt(ms + EPS) * fg_ref[...]
    o_ref[...] = x


def _combine(ye, tok_col, toff, set_id, x2d, gate2, final_g, tokens_per_mod, final):
    n, d = x2d.shape
    _, e, c, _ = ye.shape
    tm, ch = COMBINE_TM, COMBINE_CH
    n_tiles = n // tm
    bpm = tokens_per_mod // tm
    start = toff[::tm // LANES, :, 0]
    first = jnp.concatenate([start, jnp.full((1, e), c, jnp.int32)], axis=0)
    clo = first[:-1] // ch
    chi = (first[1:] + ch - 1) // ch
    ncnt = jnp.maximum(chi - clo, 0)
    max_rows = e * (tm // ch + 2) * ch
    max_rows = ((max_rows + COMBINE_GROUP - 1) // COMBINE_GROUP) * COMBINE_GROUP
    return pl.pallas_call(
        functools.partial(_combine_kernel, final=final, set_id=set_id),
        grid_spec=pltpu.PrefetchScalarGridSpec(
            num_scalar_prefetch=2,
            grid=(n_tiles,),
            in_specs=[pl.BlockSpec(memory_space=pl.ANY),
                      pl.BlockSpec(memory_space=pl.ANY),
                      pl.BlockSpec((tm, d), lambda i, a, b: (i, 0)),
                      pl.BlockSpec((None, 1, d), lambda i, a, b: (i // bpm, 0, 0)),
                      pl.BlockSpec((1, d), lambda i, a, b: (0, 0))],
            out_specs=pl.BlockSpec((tm, d), lambda i, a, b: (i, 0)),
            scratch_shapes=[pltpu.VMEM((2, max_rows, d), BF16),
                            pltpu.VMEM((2, max_rows, LANES), jnp.int32),
                            pltpu.VMEM((tm, d), F32),
                            pltpu.SemaphoreType.DMA((2,))]),
        out_shape=jax.ShapeDtypeStruct((n, d), F32),
        compiler_params=_cparams(("arbitrary",)),
        name="combine_final" if final else "combine",
    )(clo.reshape(-1), ncnt.reshape(-1), ye, tok_col, x2d, gate2, final_g.reshape(1, d))


def _rope_tables(n_tokens):
    t = jnp.arange(n_tokens)
    row = (t // GRID_W).astype(F32)
    col = (t % GRID_W).astype(F32)
    half = HEAD_DIM // 2
    inv = ROPE_BASE ** (-jnp.arange(0, half, 2, dtype=F32) / half)
    ar, ac = row[:, None] * inv, col[:, None] * inv
    c64 = jnp.concatenate([jnp.cos(ar), jnp.cos(ar), jnp.cos(ac), jnp.cos(ac)], axis=-1)
    s64 = jnp.concatenate([-jnp.sin(ar), jnp.sin(ar), -jnp.sin(ac), jnp.sin(ac)], axis=-1)
    return jnp.tile(c64, (1, 2)), jnp.tile(s64, (1, 2))


def _tile_prefix(mask3, il_ref, to_ref, tot_ref):
    nt, ne, ln = mask3.shape
    m2 = mask3.astype(F32).astype(BF16).reshape(nt * ne, ln)
    r = lax.broadcasted_iota(jnp.int32, (ln, ln), 0)
    c = lax.broadcasted_iota(jnp.int32, (ln, ln), 1)
    upper = (r <= c).astype(F32).astype(BF16)
    ones = jnp.ones((ln, ln), BF16)
    il_ref[...] = jnp.dot(m2, upper, preferred_element_type=F32).reshape(nt, ne, ln)
    tot_ref[...] = jnp.dot(m2, ones, preferred_element_type=F32).reshape(nt, ne, ln)
    run = jnp.zeros((ne, ln), F32)
    for tt in range(nt):
        to_ref[tt] = run
        run = run + tot_ref[tt]


def _split3_bf16(v):
    p1 = v.astype(BF16)
    r1 = v - p1.astype(F32)
    p2 = r1.astype(BF16)
    return p1, p2, (r1 - p2.astype(F32)).astype(BF16)


def _route_kernel(aff_ref, idx_ref, gcol_ref, tokcol_ref, toff_ref, il_ref, to_ref, tot_ref, *, cap):
    nt, ne, ln = aff_ref.shape
    a3 = aff_ref[...]

    def count(mask3):
        per_lane = jnp.sum(mask3.astype(jnp.int32), axis=0)
        return jnp.sum(per_lane, axis=1, keepdims=True)

    def as_float(thr_bits):
        return pltpu.bitcast(jnp.broadcast_to(thr_bits, (ne, ln)), F32)[:, 0:1]

    def search(i, thr_bits):
        cand = thr_bits | lax.shift_left(jnp.int32(1), 30 - i)
        return jnp.where(count(a3 >= as_float(cand)[None]) >= cap, cand, thr_bits)

    thr = as_float(lax.fori_loop(0, 31, search, jnp.zeros((ne, 1), jnp.int32)))
    gt = a3 > thr[None]
    eq = a3 == thr[None]
    need = (cap - count(gt)).astype(F32)
    _tile_prefix(eq, il_ref, to_ref, tot_ref)
    eq_before = to_ref[...] + il_ref[...] - eq.astype(F32)
    sel = gt | (eq & (eq_before < need[None]))
    _tile_prefix(sel, il_ref, to_ref, tot_ref)
    toff_ref[...] = to_ref[...].astype(jnp.int32)

    reps = cap // ln
    ones = jnp.ones((ln, ln), BF16)

    def to_column(vals, bf16_exact=False):
        out = None
        for piece in ((vals.astype(BF16),) if bf16_exact else _split3_bf16(vals)):
            t = lax.dot_general(piece, ones, (((0,), (0,)), ((), ())), preferred_element_type=F32)
            out = t if out is None else out + t
        return out

    slot = lax.broadcasted_iota(jnp.int32, (nt, cap), 1).astype(F32)
    tile_id = lax.broadcasted_iota(jnp.int32, (nt, cap), 0).astype(F32)
    in_tile = lax.broadcasted_iota(jnp.int32, (ln, cap), 0).astype(F32)
    for e in range(ne):
        il_e = il_ref[:, e, :]
        to_e = jnp.tile(to_ref[:, e, :], (1, reps))
        tot_e = jnp.tile(tot_ref[:, e, :], (1, reps))
        owns = (to_e <= slot) & (slot < to_e + tot_e)
        rank = jnp.sum(jnp.where(owns, slot - to_e, 0.0), axis=0, keepdims=True)
        tile_of = jnp.sum(jnp.where(owns, tile_id, 0.0), axis=0, keepdims=True)
        owns_b = owns.astype(F32).astype(BF16)
        contract0 = (((0,), (0,)), ((), ()))
        il_g = lax.dot_general(il_e.astype(BF16), owns_b, contract0,
                               preferred_element_type=F32)
        pos = jnp.sum((il_g <= rank).astype(F32), axis=0, keepdims=True)
        idx_ref[e:e + 1, :] = (tile_of * ln + pos).astype(jnp.int32)
        p1, p2, p3 = _split3_bf16(aff_ref[:, e, :])
        a_g = (lax.dot_general(p1, owns_b, contract0, preferred_element_type=F32)
               + lax.dot_general(p2, owns_b, contract0, preferred_element_type=F32)
               + lax.dot_general(p3, owns_b, contract0, preferred_element_type=F32))
        hit = in_tile == pos
        gcol_ref[e] = to_column(jnp.where(hit, a_g, 0.0))
        tok_hi = to_column(jnp.where(hit, tile_of, 0.0), bf16_exact=True)
        tok_lo = to_column(jnp.where(hit, pos, 0.0), bf16_exact=True)
        tokcol_ref[e] = (tok_hi * ln + tok_lo).astype(jnp.int32)


def _route(aff_all, cap):
    s, nt, ne, ln = aff_all.shape
    assert nt <= 256 and cap % ln == 0
    return pl.pallas_call(
        functools.partial(_route_kernel, cap=cap),
        grid=(s,),
        in_specs=[pl.BlockSpec((None, nt, ne, ln), lambda i: (i, 0, 0, 0))],
        out_specs=[pl.BlockSpec((None, ne, cap), lambda i: (i, 0, 0)),
                   pl.BlockSpec((None, ne, cap, ln), lambda i: (i, 0, 0, 0)),
                   pl.BlockSpec((None, ne, cap, ln), lambda i: (i, 0, 0, 0)),
                   pl.BlockSpec((None, nt, ne, ln), lambda i: (i, 0, 0, 0))],
        out_shape=[jax.ShapeDtypeStruct((s, ne, cap), jnp.int32),
                   jax.ShapeDtypeStruct((s, ne, cap, ln), F32),
                   jax.ShapeDtypeStruct((s, ne, cap, ln), jnp.int32),
                   jax.ShapeDtypeStruct((s, nt, ne, ln), jnp.int32)],
        scratch_shapes=[pltpu.VMEM((nt, ne, ln), F32)] * 3,
        compiler_params=_cparams(("parallel",)),
        name="route",
    )(aff_all)


def _route_and_ffn(h2_list, aff_list, w_gate, w_up, w_down, layer):
    n = h2_list[0].shape[0]
    assert all(h.shape[0] == n for h in h2_list)
    cap = EC_CAPACITY * n // N_EXPERTS
    idx_all, gcol, tokcol, toff = _route(jnp.stack(aff_list), cap)
    ye = _expert_ffn(idx_all.reshape(-1), h2_list[0], h2_list[1], w_gate, w_up, w_down, gcol, layer)
    return ye, tokcol, toff


def kernel(x_prompt, x_sample, cache_k, cache_v, c, c_ctx, w_mod, b_mod, norm1_g, norm2_g,
           attn_wqkv, attn_wo, lambda_q1, lambda_k1, lambda_q2, lambda_k2, subln_g,
           pool_w, pool_b, pool_scale, router_w, expert_w_gate, expert_w_up, expert_w_down,
           final_g):
    bp, n_ctx, d = x_prompt.shape
    bs, n_lat, _ = x_sample.shape
    past = cache_k.shape[2]
    n_p, n_s = bp * n_ctx, bs * n_lat

    rows = 8 * ((1 + bs + 7) // 8)
    cvecs = jnp.zeros((rows, d), F32).at[0].set(c_ctx).at[1:1 + bs].set(c)
    mod = _modulation_all(cvecs, w_mod, b_mod)
    rope_tabs = _rope_tables(n_lat)
    cache_k4 = cache_k.reshape(bs, cache_k.shape[1], past, ATTN_WIDTH)
    cache_v4 = cache_v.reshape(bs, cache_v.shape[1], past, ATTN_WIDTH)
    w_gate_b = expert_w_gate.astype(BF16)
    w_up_b = expert_w_up.astype(BF16)
    w_down_b = expert_w_down.astype(BF16)

    xp = x_prompt.reshape(n_p, d)
    xs = x_sample.reshape(n_s, d)
    new_k, new_v = [], []
    for l in range(DEPTH):
        mp_ = mod[l, 0:1].reshape(1, 6, 1, d)
        ms_ = mod[l, 1:1 + bs].reshape(bs, 6, 1, d)
        sp1, cp1, gp1, sp2, cp2, gp2 = [mp_[:, i] for i in range(6)]
        ss1, cs1, gs1, ss2, cs2, gs2 = [ms_[:, i] for i in range(6)]
        wr_t = router_w[l].T.astype(BF16)
        if l % 2 == 0:
            a = l // 2
            lam_init = 0.8 - 0.6 * math.exp(-0.3 * l)
            lamv = jnp.stack([lambda_q1[a], lambda_k1[a], lambda_q2[a], lambda_k2[a]])
            wqkv = attn_wqkv[a].astype(BF16)
            wo = attn_wo[a].astype(BF16)
            qp, kp, vp = _qkv_project(xp, norm1_g[l], sp1, cp1, wqkv, None, n_p, F32)
            new_k.append(kp.reshape(bp, n_ctx, N_HEADS, 2 * HEAD_DIM))
            new_v.append(vp.reshape(bp, n_ctx, N_HEADS, V_DIM))
            op = _diff_attention(qp, kp, vp, lamv, subln_g[a], lam_init, bp)
            xp, h2p, affp = _proj_mixer(op, wo, xp, gp1, norm2_g[l], sp2, cp2, wr_t, n_p)
            qs, ks_, vs = _qkv_project(xs, norm1_g[l], ss1, cs1, wqkv, rope_tabs, n_lat, BF16)
            os_ = _diff_attention(qs, ks_, vs, lamv, subln_g[a], lam_init, bs,
                                  cache=(cache_k4, cache_v4, a))
            xs, h2s, affs = _proj_mixer(os_, wo, xs, gs1, norm2_g[l], ss2, cs2, wr_t, n_lat)
        else:
            p = l // 2
            xp, h2p, affp = _pool_mixer(xp, norm1_g[l], sp1, cp1, pool_w[p], pool_b[p],
                                        pool_scale[p], gp1, norm2_g[l], sp2, cp2, wr_t,
                                        n_p, n_ctx)
            xs, h2s, affs = _pool_mixer(xs, norm1_g[l], ss1, cs1, pool_w[p], pool_b[p],
                                        pool_scale[p], gs1, norm2_g[l], ss2, cs2, wr_t,
                                        n_lat, n_lat)
        ye, tokcol, toff = _route_and_ffn([h2p, h2s], [affp, affs], w_gate_b, w_up_b, w_down_b, l)
        last = l == DEPTH - 1
        xp = _combine(ye, tokcol, toff[0], 0, xp, gp2, final_g, n_p, last)
        xs = _combine(ye, tokcol, toff[1], 1, xs, gs2, final_g, n_lat, last)
    return (xp.reshape(bp, n_ctx, d), xs.reshape(bs, n_lat, d),
            jnp.stack(new_k, axis=1), jnp.stack(new_v, axis=1))
```

```python
import math
import functools
import jax
import jax.numpy as jnp
from jax import lax
from jax.experimental import pallas as pl
from jax.experimental.pallas import tpu as pltpu

D_MODEL = 1024
DEPTH = 4
GRID_W = 64
N_HEADS = 8
HEAD_DIM = 64
V_DIM = 2 * HEAD_DIM
ATTN_WIDTH = N_HEADS * V_DIM
ROPE_BASE = 10000.0
N_POOL_GROUPS = 4
POOL_GROUP = D_MODEL // N_POOL_GROUPS
POOL_WINDOWS = (2, 4, 8, 16)
POOL_HALO = 8
N_EXPERTS = 16
EC_CAPACITY = 2
EPS = 1e-6

F32 = jnp.float32
BF16 = jnp.bfloat16
LANES = 128
LOG2E = 1.4426950408889634
VMEM_LIMIT = 56 * 1024 * 1024
ROW_TILE = 512
ATTN_TQ = 512
ATTN_TK = 4096
ATTN_HEADS_PER_STEP = 2
ATTN_SHORT_KEYS = 512
FFN_ROWS = 512
SUBLANES = 8
COMBINE_TM = 256
COMBINE_CH = 32
COMBINE_GROUP = 256


def _cparams(sem):
    return pltpu.CompilerParams(dimension_semantics=sem, vmem_limit_bytes=VMEM_LIMIT)


def _mod_kernel(c_ref, w_ref, b_ref, o_ref):
    c = c_ref[...]
    s = (c * (1.0 / (1.0 + jnp.exp(-c)))).astype(BF16)
    o_ref[...] = jnp.dot(s, w_ref[...].astype(BF16), preferred_element_type=F32) + b_ref[...]


def _modulation_all(cvecs, w_mod, b_mod):
    r, d = cvecs.shape
    depth, _, n6 = w_mod.shape
    tn = 1024
    return pl.pallas_call(
        _mod_kernel,
        grid=(depth, n6 // tn),
        in_specs=[pl.BlockSpec((r, d), lambda l, j: (0, 0)),
                  pl.BlockSpec((None, d, tn), lambda l, j: (l, 0, j)),
                  pl.BlockSpec((None, 1, tn), lambda l, j: (l, 0, j))],
        out_specs=pl.BlockSpec((None, r, tn), lambda l, j: (l, 0, j)),
        out_shape=jax.ShapeDtypeStruct((depth, r, n6), F32),
        compiler_params=_cparams(("parallel", "parallel")),
        name="modulation",
    )(cvecs, w_mod, b_mod.reshape(depth, 1, n6))


def _adaln_rows(x, g, shift, scale):
    ms = jnp.mean(x * x, axis=-1, keepdims=True)
    return (x * lax.rsqrt(ms + EPS)) * g * (1.0 + scale) + shift


def _swap16(x):
    lane = lax.broadcasted_iota(jnp.int32, x.shape, 1)
    up = pltpu.roll(x, x.shape[1] - 16, axis=1)
    dn = pltpu.roll(x, 16, axis=1)
    return jnp.where((lane % 32) < 16, up, dn)


def _qkv_kernel(*refs, rope):
    if rope:
        x_ref, g_ref, sh_ref, sc_ref, w_ref, cos_ref, sin_ref, q_ref, k_ref, v_ref = refs
    else:
        x_ref, g_ref, sh_ref, sc_ref, w_ref, q_ref, k_ref, v_ref = refs
    h = _adaln_rows(x_ref[...], g_ref[...], sh_ref[...], sc_ref[...]).astype(BF16)
    width = q_ref.shape[1]
    qscale = (HEAD_DIM ** -0.5) * LOG2E
    for part, o_ref in enumerate((q_ref, k_ref, v_ref)):
        r = jnp.dot(h, w_ref[:, part * width:(part + 1) * width], preferred_element_type=F32)
        if part < 2 and rope:
            c = cos_ref[...]
            s = sin_ref[...]
            for hd in range(width // LANES):
                blk = r[:, hd * LANES:(hd + 1) * LANES]
                blk = blk * c + _swap16(blk) * s
                if part == 0:
                    blk = blk * qscale
                o_ref[:, hd * LANES:(hd + 1) * LANES] = blk.astype(o_ref.dtype)
        else:
            if part == 0:
                r = r * qscale
            o_ref[...] = r.astype(o_ref.dtype)


def _qkv_project(x2d, g, shift, scale, w_bf16, rope_tabs, tokens_per_mod, kv_dtype):
    n, d = x2d.shape
    tm = ROW_TILE
    width = w_bf16.shape[1] // 3
    rope = rope_tabs is not None
    blocks_per_mod = tokens_per_mod // tm
    in_specs = [pl.BlockSpec((tm, d), lambda i: (i, 0)),
                pl.BlockSpec((1, d), lambda i: (0, 0)),
                pl.BlockSpec((None, 1, d), lambda i: (i // blocks_per_mod, 0, 0)),
                pl.BlockSpec((None, 1, d), lambda i: (i // blocks_per_mod, 0, 0)),
                pl.BlockSpec((d, 3 * width), lambda i: (0, 0))]
    args = [x2d, g.reshape(1, d), shift, scale, w_bf16]
    if rope:
        cos_t, sin_t = rope_tabs
        seq_blocks = cos_t.shape[0] // tm
        in_specs += [pl.BlockSpec((tm, LANES), lambda i: (i % seq_blocks, 0)),
                     pl.BlockSpec((tm, LANES), lambda i: (i % seq_blocks, 0))]
        args += [cos_t, sin_t]
    out_spec = pl.BlockSpec((tm, width), lambda i: (i, 0))
    return pl.pallas_call(
        functools.partial(_qkv_kernel, rope=rope),
        grid=(n // tm,),
        in_specs=in_specs,
        out_specs=[out_spec, out_spec, out_spec],
        out_shape=[jax.ShapeDtypeStruct((n, width), BF16),
                   jax.ShapeDtypeStruct((n, width), kv_dtype),
                   jax.ShapeDtypeStruct((n, width), kv_dtype)],
        compiler_params=_cparams(("parallel",)),
        name="adaln_qkv_rope" if rope else "adaln_qkv",
    )(*args)


def _attn_kernel(*refs, tk, lam_init, hp, cached):
    if cached:
        (lamv_ref, q_ref, k_ref, v_ref, kc_ref, vc_ref, g_ref, o_ref,
         qs_ref, m_ref, l_ref, acc_ref) = refs
    else:
        lamv_ref, q_ref, k_ref, v_ref, g_ref, o_ref, qs_ref, m_ref, l_ref, acc_ref = refs
    tq = q_ref.shape[0]
    nk = k_ref.shape[0]
    for h in range(hp):
        q = q_ref[:, h * LANES:(h + 1) * LANES]
        lane = lax.broadcasted_iota(jnp.int32, q.shape, 1)
        zero = jnp.zeros_like(q)
        qs_ref[h, 0:tq, :] = jnp.where(lane < HEAD_DIM, q, zero)
        qs_ref[h, tq:2 * tq, :] = jnp.where(lane >= HEAD_DIM, q, zero)
    m_ref[...] = jnp.full(m_ref.shape, -jnp.inf, F32)
    l_ref[...] = jnp.zeros(l_ref.shape, F32)
    acc_ref[...] = jnp.zeros(acc_ref.shape, F32)

    def online_softmax_step(h, kc, vc):
        s = lax.dot_general(qs_ref[h], kc, (((1,), (1,)), ((), ())),
                            preferred_element_type=F32)
        m_old = m_ref[h]
        m_new = jnp.maximum(m_old, jnp.max(s, axis=1)[:, None])
        alpha = jnp.exp2(m_old - m_new)
        p = jnp.exp2(s - jnp.tile(m_new, (1, s.shape[1] // LANES)))
        l_ref[h] = alpha * l_ref[h] + jnp.sum(p, axis=1)[:, None]
        acc_ref[h] = alpha * acc_ref[h] + jnp.dot(p.astype(BF16), vc, preferred_element_type=F32)
        m_ref[h] = m_new

    def body(j, carry):
        off = pl.multiple_of(j * tk, tk)
        for h in range(hp):
            online_softmax_step(h, k_ref[pl.ds(off, tk), h * LANES:(h + 1) * LANES].astype(BF16),
                                v_ref[pl.ds(off, tk), h * LANES:(h + 1) * LANES].astype(BF16))
        return carry

    lax.fori_loop(0, nk // tk, body, 0)
    if cached:
        for h in range(hp):
            online_softmax_step(h, kc_ref[:, h * LANES:(h + 1) * LANES].astype(BF16),
                                vc_ref[:, h * LANES:(h + 1) * LANES].astype(BF16))

    lv = lamv_ref[...]
    lam = (jnp.exp(jnp.sum(lv[0:1, :] * lv[1:2, :], axis=-1, keepdims=True))
           - jnp.exp(jnp.sum(lv[2:3, :] * lv[3:4, :], axis=-1, keepdims=True)) + lam_init)
    for h in range(hp):
        o = acc_ref[h] / l_ref[h]
        d = o[0:tq, :] - lam * o[tq:2 * tq, :]
        ms = jnp.mean(d * d, axis=-1, keepdims=True)
        o_ref[:, h * LANES:(h + 1) * LANES] = (
            (d * lax.rsqrt(ms + EPS)) * g_ref[...] * (1.0 - lam_init)).astype(o_ref.dtype)


def _diff_attention(q, k, v, lamv, subln_g, lam_init, batch, cache=None):
    nq = q.shape[0] // batch
    nk = k.shape[0] // batch
    tq = min(ATTN_TQ, nq)
    tk = min(ATTN_TK, nk)
    assert nq % tq == 0 and nk % tk == 0 and tk % LANES == 0
    hp = N_HEADS if nk <= ATTN_SHORT_KEYS else ATTN_HEADS_PER_STEP
    w = hp * LANES
    q3 = q.reshape(batch, nq, ATTN_WIDTH)
    k3 = k.reshape(batch, nk, ATTN_WIDTH)
    v3 = v.reshape(batch, nk, ATTN_WIDTH)
    kv_spec = pl.BlockSpec((None, nk, w), lambda b, h, i: (b, 0, h))
    in_specs = [pl.BlockSpec((4, HEAD_DIM), lambda b, h, i: (0, 0)),
                pl.BlockSpec((None, tq, w), lambda b, h, i: (b, i, h)), kv_spec, kv_spec]
    args = [lamv, q3, k3, v3]
    if cache is not None:
        ck, cv, layer = cache
        past = ck.shape[2]
        assert past % LANES == 0
        c_spec = pl.BlockSpec((None, None, past, w), lambda b, h, i: (b, layer, 0, h))
        in_specs += [c_spec, c_spec]
        args += [ck, cv]
    in_specs.append(pl.BlockSpec((1, V_DIM), lambda b, h, i: (0, 0)))
    args.append(subln_g.reshape(1, V_DIM))
    out = pl.pallas_call(
        functools.partial(_attn_kernel, tk=tk, lam_init=lam_init, hp=hp, cached=cache is not None),
        grid=(batch, N_HEADS // hp, nq // tq),
        in_specs=in_specs,
        out_specs=pl.BlockSpec((None, tq, w), lambda b, h, i: (b, i, h)),
        out_shape=jax.ShapeDtypeStruct((batch, nq, ATTN_WIDTH), BF16),
        scratch_shapes=[pltpu.VMEM((hp, 2 * tq, V_DIM), BF16),
                        pltpu.VMEM((hp, 2 * tq, LANES), F32),
                        pltpu.VMEM((hp, 2 * tq, LANES), F32),
                        pltpu.VMEM((hp, 2 * tq, V_DIM), F32)],
        compiler_params=_cparams(("parallel", "parallel", "parallel")),
        name="diff_attention",
    )(*args)
    return out.reshape(batch * nq, ATTN_WIDTH)


def _epilogue(x_new, g2, sh2, sc2, wr_t, xo_ref, h2_ref, aff_ref):
    xo_ref[...] = x_new
    h2 = _adaln_rows(x_new, g2, sh2, sc2)
    h2b = h2.astype(BF16)
    h2_ref[...] = h2
    logits = lax.dot_general(wr_t, h2b, (((1,), (1,)), ((), ())),
                             preferred_element_type=F32)
    mx = jnp.max(logits, axis=0, keepdims=True)
    ex = jnp.exp(logits - mx)
    aff = ex / jnp.sum(ex, axis=0, keepdims=True)
    for r in range(aff_ref.shape[0]):
        aff_ref[r] = aff[:, r * LANES:(r + 1) * LANES]


def _proj_kernel(o_ref, w_ref, x_ref, gate_ref, g2_ref, sh2_ref, sc2_ref, wr_ref,
                 xo_ref, h2_ref, aff_ref):
    y = jnp.dot(o_ref[...], w_ref[...], preferred_element_type=F32)
    x_new = x_ref[...] + gate_ref[...] * y
    _epilogue(x_new, g2_ref[...], sh2_ref[...], sc2_ref[...], wr_ref[...], xo_ref, h2_ref, aff_ref)


def _epilogue_specs(d, tm, blocks_per_mod):
    mod_spec = pl.BlockSpec((None, 1, d), lambda i: (i // blocks_per_mod, 0, 0))
    in_specs = [pl.BlockSpec((1, d), lambda i: (0, 0)), mod_spec, mod_spec,
                pl.BlockSpec((N_EXPERTS, d), lambda i: (0, 0))]
    out_specs = [pl.BlockSpec((tm, d), lambda i: (i, 0)),
                 pl.BlockSpec((tm, d), lambda i: (i, 0)),
                 pl.BlockSpec((tm // LANES, N_EXPERTS, LANES), lambda i: (i, 0, 0))]
    return in_specs, out_specs


def _epilogue_out_shape(n, d):
    return [jax.ShapeDtypeStruct((n, d), F32),
            jax.ShapeDtypeStruct((n, d), F32),
            jax.ShapeDtypeStruct((n // LANES, N_EXPERTS, LANES), F32)]


def _proj_mixer(o, w_bf16, x2d, gate1, g2, sh2, sc2, wr_t, tokens_per_mod):
    n, d = x2d.shape
    tm = ROW_TILE
    kdim = o.shape[1]
    bpm = tokens_per_mod // tm
    ep_in, ep_out = _epilogue_specs(d, tm, bpm)
    return pl.pallas_call(
        _proj_kernel,
        grid=(n // tm,),
        in_specs=[pl.BlockSpec((tm, kdim), lambda i: (i, 0)),
                  pl.BlockSpec((kdim, d), lambda i: (0, 0)),
                  pl.BlockSpec((tm, d), lambda i: (i, 0)),
                  pl.BlockSpec((None, 1, d), lambda i: (i // bpm, 0, 0))] + ep_in,
        out_specs=ep_out,
        out_shape=_epilogue_out_shape(n, d),
        compiler_params=_cparams(("parallel",)),
        name="proj_mixer",
    )(o, w_bf16, x2d, gate1, g2.reshape(1, d), sh2, sc2, wr_t)


def _pool_kernel(x_ref, xprev_ref, xnext_ref, g1_ref, sh1_ref, sc1_ref, pw_ref, pb_ref, ps_ref,
                 gate_ref, g2_ref, sh2_ref, sc2_ref, wr_ref, xo_ref, h2_ref, aff_ref, *, seq_len):
    tm, d = x_ref.shape
    halo = POOL_HALO
    i = pl.program_id(0)
    pos0 = (i * tm) % seq_len
    g1, sh1, sc1 = g1_ref[...], sh1_ref[...], sc1_ref[...]
    x = x_ref[...]
    h = _adaln_rows(x, g1, sh1, sc1)
    hprev = _adaln_rows(xprev_ref[...], g1, sh1, sc1)
    hnext = _adaln_rows(xnext_ref[...], g1, sh1, sc1)
    hprev = jnp.where(pos0 > 0, hprev, 0.0)
    hnext = jnp.where(pos0 + tm < seq_len, hnext, 0.0)
    hext = jnp.concatenate([hprev, h, hnext], axis=0)
    rows = tm + 2 * halo
    t = pos0 + lax.broadcasted_iota(jnp.int32, (tm, 1), 0)
    outs = []
    for gi, win in enumerate(POOL_WINDOWS):
        a = hext[:, gi * POOL_GROUP:(gi + 1) * POOL_GROUP]
        p = pltpu.roll(a, 1, axis=0) + a
        step = 1
        while 2 * step < win:
            p = pltpu.roll(p, step, axis=0) + pltpu.roll(p, rows - step, axis=0)
            step *= 2
        half = win // 2
        cnt = (jnp.minimum(t + half, seq_len) - jnp.maximum(t - half, 0)).astype(F32)
        pooled = p[halo:halo + tm, :] / cnt - a[halo:halo + tm, :]
        y = jnp.dot(pooled.astype(BF16), pw_ref[gi].astype(BF16), preferred_element_type=F32)
        outs.append(y + pb_ref[gi:gi + 1, :])
    mix = jnp.concatenate(outs, axis=1) * ps_ref[...]
    x_new = x + gate_ref[...] * mix
    _epilogue(x_new, g2_ref[...], sh2_ref[...], sc2_ref[...], wr_ref[...], xo_ref, h2_ref, aff_ref)


def _pool_mixer(x2d, g1, sh1, sc1, pool_w, pool_b, pool_scale, gate1, g2, sh2, sc2, wr_t,
                tokens_per_mod, seq_len):
    n, d = x2d.shape
    tm = min(ROW_TILE, seq_len)
    bpm = tokens_per_mod // tm
    hb = tm // POOL_HALO
    n_hblocks = n // POOL_HALO
    mod_spec = pl.BlockSpec((None, 1, d), lambda i: (i // bpm, 0, 0))
    ep_in, ep_out = _epilogue_specs(d, tm, bpm)
    return pl.pallas_call(
        functools.partial(_pool_kernel, seq_len=seq_len),
        grid=(n // tm,),
        in_specs=[pl.BlockSpec((tm, d), lambda i: (i, 0)),
                  pl.BlockSpec((POOL_HALO, d), lambda i: (jnp.maximum(i * hb - 1, 0), 0)),
                  pl.BlockSpec((POOL_HALO, d),
                               lambda i: (jnp.minimum((i + 1) * hb, n_hblocks - 1), 0)),
                  pl.BlockSpec((1, d), lambda i: (0, 0)), mod_spec, mod_spec,
                  pl.BlockSpec((N_POOL_GROUPS, POOL_GROUP, POOL_GROUP), lambda i: (0, 0, 0)),
                  pl.BlockSpec((N_POOL_GROUPS, POOL_GROUP), lambda i: (0, 0)),
                  pl.BlockSpec((1, d), lambda i: (0, 0)),
                  mod_spec] + ep_in,
        out_specs=ep_out,
        out_shape=_epilogue_out_shape(n, d),
        compiler_params=_cparams(("parallel",)),
        name="pool_mixer",
    )(x2d, x2d, x2d, g1.reshape(1, d), sh1, sc1, pool_w, pool_b, pool_scale.reshape(1, d),
      gate1, g2.reshape(1, d), sh2, sc2, wr_t)


def _ffn_kernel(idx_ref, h2a_hbm, h2b_hbm, wg_ref, wu_ref, wd_ref, g_ref, ye_ref, xbuf, sem):
    n_e, n_s = pl.num_programs(0), pl.num_programs(1)
    ei, si = pl.program_id(0), pl.program_id(1)
    n_blocks, sub = xbuf.shape[1], xbuf.shape[2]
    cap = n_blocks * sub
    rows = min(FFN_ROWS, cap)
    step = ei * n_s + si
    slot = step % 2

    def issue(e_t, s_t, slot_t):
        base = (s_t * n_e + e_t) * cap
        for set_id, src in enumerate((h2a_hbm, h2b_hbm)):
            @pl.when(s_t == set_id)
            def _(src=src):
                def body(kb, carry):
                    k0 = kb * sub
                    for u in range(sub):
                        tok = idx_ref[base + k0 + u]
                        pltpu.async_copy(src.at[lax.shift_right_logical(tok, 3), pl.ds(tok & 7, 1)],
                                         xbuf.at[slot_t, kb, pl.ds(u, 1)], sem.at[slot_t],
                                         priority=u % 2)
                    return carry

                lax.fori_loop(0, n_blocks, body, 0)

    @pl.when(step == 0)
    def _():
        issue(ei, si, slot)

    @pl.when(step + 1 < n_e * n_s)
    def _():
        nxt = step + 1
        issue(nxt // n_s, nxt % n_s, 1 - slot)

    def wait_body(kb, carry):
        pltpu.make_async_copy(h2a_hbm.at[0], xbuf.at[slot, 0], sem.at[slot]).wait()
        return carry

    lax.fori_loop(0, n_blocks, wait_body, 0)

    def body(i, carry):
        r0 = pl.multiple_of(i * rows, rows)
        b0 = pl.multiple_of(i * (rows // sub), rows // sub)
        x = xbuf[slot, pl.ds(b0, rows // sub)].reshape(rows, xbuf.shape[3]).astype(BF16)
        a = jnp.dot(x, wg_ref[...], preferred_element_type=F32)
        b = jnp.dot(x, wu_ref[...], preferred_element_type=F32)
        hid = (a * (1.0 / (1.0 + jnp.exp(-a))) * b).astype(BF16)
        y = jnp.dot(hid, wd_ref[...], preferred_element_type=F32)
        gate = g_ref[pl.ds(r0, rows), :]
        for j in range(y.shape[1] // LANES):
            ye_ref[pl.ds(r0, rows), j * LANES:(j + 1) * LANES] = (
                y[:, j * LANES:(j + 1) * LANES] * gate).astype(ye_ref.dtype)
        return carry

    lax.fori_loop(0, cap // rows, body, 0)


def _expert_ffn(idx_flat, h2a, h2b, wg, wu, wd, gates, layer):
    s, e, c, _ = gates.shape
    assert s == 2
    d, f = wg.shape[2], wg.shape[3]
    return pl.pallas_call(
        _ffn_kernel,
        grid_spec=pltpu.PrefetchScalarGridSpec(
            num_scalar_prefetch=1,
            grid=(e, s),
            in_specs=[pl.BlockSpec(memory_space=pl.ANY),
                      pl.BlockSpec(memory_space=pl.ANY),
                      pl.BlockSpec((None, None, d, f), lambda ei, si, idx: (layer, ei, 0, 0)),
                      pl.BlockSpec((None, None, d, f), lambda ei, si, idx: (layer, ei, 0, 0)),
                      pl.BlockSpec((None, None, f, d), lambda ei, si, idx: (layer, ei, 0, 0)),
                      pl.BlockSpec((None, None, c, LANES), lambda ei, si, idx: (si, ei, 0, 0))],
            out_specs=pl.BlockSpec((None, None, c, d), lambda ei, si, idx: (si, ei, 0, 0)),
            scratch_shapes=[pltpu.VMEM((2, c // SUBLANES, SUBLANES, d), F32),
                            pltpu.SemaphoreType.DMA((2,))]),
        out_shape=jax.ShapeDtypeStruct((s, e, c, d), BF16),
        compiler_params=pltpu.CompilerParams(dimension_semantics=("arbitrary", "arbitrary"),
                                             vmem_limit_bytes=VMEM_LIMIT,
                                             disable_bounds_checks=True),
        name="expert_ffn",
    )(idx_flat, h2a.reshape(-1, SUBLANES, d), h2b.reshape(-1, SUBLANES, d), wg, wu, wd, gates)


def _combine_kernel(clo_ref, ncnt_ref, ye_hbm, tok_hbm, x_ref, gate_ref, fg_ref, o_ref,
                    zbuf, tokbuf, acc_ref, sem, *, final, set_id):
    n_e = ye_hbm.shape[1]
    tm, d = x_ref.shape
    ch, grp = COMBINE_CH, COMBINE_GROUP
    i = pl.program_id(0)
    nt = pl.num_programs(0)
    slot = i % 2

    def chunk_copies(e, src_row, slot_t, dst_row):
        return (pltpu.make_async_copy(ye_hbm.at[set_id, e, pl.ds(src_row, ch), :],
                                      zbuf.at[slot_t, pl.ds(dst_row, ch), :], sem.at[slot_t]),
                pltpu.make_async_copy(tok_hbm.at[set_id, e, pl.ds(src_row, ch), :],
                                      tokbuf.at[slot_t, pl.ds(dst_row, ch), :], sem.at[slot_t]))

    def issue(tile, slot_t):
        pos = 0
        for e in range(n_e):
            lo = clo_ref[tile * n_e + e]

            def body(j, p, e=e, lo=lo):
                src = pl.multiple_of((lo + j) * ch, ch)
                dst = pl.multiple_of(p * ch, ch)
                for cp in chunk_copies(e, src, slot_t, dst):
                    cp.start()
                return p + 1

            pos = lax.fori_loop(0, ncnt_ref[tile * n_e + e], body, pos)

    def total_chunks(tile):
        tot = 0
        for e in range(n_e):
            tot = tot + ncnt_ref[tile * n_e + e]
        return tot

    @pl.when(i == 0)
    def _():
        zbuf[...] = jnp.zeros(zbuf.shape, zbuf.dtype)
        tokbuf[...] = jnp.full(tokbuf.shape, -1, jnp.int32)
        issue(i, slot)

    @pl.when(i + 1 < nt)
    def _():
        issue(i + 1, 1 - slot)

    total = total_chunks(i)

    def wait_body(j, carry):
        for cp in chunk_copies(0, 0, slot, 0):
            cp.wait()
        return carry

    lax.fori_loop(0, total, wait_body, 0)

    acc_ref[...] = jnp.zeros(acc_ref.shape, F32)
    tok_of_lane = i * tm + lax.broadcasted_iota(jnp.int32, (grp, tm), 1)
    row_in_group = lax.broadcasted_iota(jnp.int32, (grp, 1), 0)

    def group_body(gidx, carry):
        r0 = pl.multiple_of(gidx * grp, grp)
        tok = jnp.where(r0 + row_in_group < total * ch, tokbuf[slot, pl.ds(r0, grp), :], -1)
        tok = jnp.tile(tok, (1, tm // LANES))
        onehot_t = (tok == tok_of_lane).astype(BF16)
        acc_ref[...] += lax.dot_general(onehot_t, zbuf[slot, pl.ds(r0, grp), :],
                                        (((0,), (0,)), ((), ())), preferred_element_type=F32)
        return carry

    lax.fori_loop(0, (total * ch + grp - 1) // grp, group_body, 0)

    x = x_ref[...] + gate_ref[...] * acc_ref[...]
    if final:
        ms = jnp.mean(x * x, axis=-1, keepdims=True)
        x = x * lax.rsqrt(ms + EPS) * fg_ref[...]
    o_ref[...] = x


def _combine(ye, tok_col, toff, set_id, x2d, gate2, final_g, tokens_per_mod, final):
    n, d = x2d.shape
    _, e, c, _ = ye.shape
    tm, ch = COMBINE_TM, COMBINE_CH
    n_tiles = n // tm
    bpm = tokens_per_mod // tm
    start = toff[::tm // LANES, :, 0]
    first = jnp.concatenate([start, jnp.full((1, e), c, jnp.int32)], axis=0)
    clo = first[:-1] // ch
    chi = (first[1:] + ch - 1) // ch
    ncnt = jnp.maximum(chi - clo, 0)
    max_rows = e * (tm // ch + 2) * ch
    max_rows = ((max_rows + COMBINE_GROUP - 1) // COMBINE_GROUP) * COMBINE_GROUP
    return pl.pallas_call(
        functools.partial(_combine_kernel, final=final, set_id=set_id),
        grid_spec=pltpu.PrefetchScalarGridSpec(
            num_scalar_prefetch=2,
            grid=(n_tiles,),
            in_specs=[pl.BlockSpec(memory_space=pl.ANY),
                      pl.BlockSpec(memory_space=pl.ANY),
                      pl.BlockSpec((tm, d), lambda i, a, b: (i, 0)),
                      pl.BlockSpec((None, 1, d), lambda i, a, b: (i // bpm, 0, 0)),
                      pl.BlockSpec((1, d), lambda i, a, b: (0, 0))],
            out_specs=pl.BlockSpec((tm, d), lambda i, a, b: (i, 0)),
            scratch_shapes=[pltpu.VMEM((2, max_rows, d), BF16),
                            pltpu.VMEM((2, max_rows, LANES), jnp.int32),
                            pltpu.VMEM((tm, d), F32),
                            pltpu.SemaphoreType.DMA((2,))]),
        out_shape=jax.ShapeDtypeStruct((n, d), F32),
        compiler_params=_cparams(("arbitrary",)),
        name="combine_final" if final else "combine",
    )(clo.reshape(-1), ncnt.reshape(-1), ye, tok_col, x2d, gate2, final_g.reshape(1, d))


def _rope_tables(n_tokens):
    t = jnp.arange(n_tokens)
    row = (t // GRID_W).astype(F32)
    col = (t % GRID_W).astype(F32)
    half = HEAD_DIM // 2
    inv = ROPE_BASE ** (-jnp.arange(0, half, 2, dtype=F32) / half)
    ar, ac = row[:, None] * inv, col[:, None] * inv
    c64 = jnp.concatenate([jnp.cos(ar), jnp.cos(ar), jnp.cos(ac), jnp.cos(ac)], axis=-1)
    s64 = jnp.concatenate([-jnp.sin(ar), jnp.sin(ar), -jnp.sin(ac), jnp.sin(ac)], axis=-1)
    return jnp.tile(c64, (1, 2)), jnp.tile(s64, (1, 2))


def _tile_prefix(mask3, il_ref, to_ref, tot_ref):
    nt, ne, ln = mask3.shape
    m2 = mask3.astype(F32).astype(BF16).reshape(nt * ne, ln)
    r = lax.broadcasted_iota(jnp.int32, (ln, ln), 0)
    c = lax.broadcasted_iota(jnp.int32, (ln, ln), 1)
    upper = (r <= c).astype(F32).astype(BF16)
    ones = jnp.ones((ln, ln), BF16)
    il_ref[...] = jnp.dot(m2, upper, preferred_element_type=F32).reshape(nt, ne, ln)
    tot_ref[...] = jnp.dot(m2, ones, preferred_element_type=F32).reshape(nt, ne, ln)
    run = jnp.zeros((ne, ln), F32)
    for tt in range(nt):
        to_ref[tt] = run
        run = run + tot_ref[tt]


def _split3_bf16(v):
    p1 = v.astype(BF16)
    r1 = v - p1.astype(F32)
    p2 = r1.astype(BF16)
    return p1, p2, (r1 - p2.astype(F32)).astype(BF16)


def _route_kernel(aff_ref, idx_ref, gcol_ref, tokcol_ref, toff_ref, il_ref, to_ref, tot_ref, *, cap):
    nt, ne, ln = aff_ref.shape
    a3 = aff_ref[...]

    def count(mask3):
        per_lane = jnp.sum(mask3.astype(jnp.int32), axis=0)
        return jnp.sum(per_lane, axis=1, keepdims=True)

    def as_float(thr_bits):
        return pltpu.bitcast(jnp.broadcast_to(thr_bits, (ne, ln)), F32)[:, 0:1]

    def search(i, thr_bits):
        cand = thr_bits | lax.shift_left(jnp.int32(1), 30 - i)
        return jnp.where(count(a3 >= as_float(cand)[None]) >= cap, cand, thr_bits)

    thr = as_float(lax.fori_loop(0, 31, search, jnp.zeros((ne, 1), jnp.int32)))
    gt = a3 > thr[None]
    eq = a3 == thr[None]
    need = (cap - count(gt)).astype(F32)
    _tile_prefix(eq, il_ref, to_ref, tot_ref)
    eq_before = to_ref[...] + il_ref[...] - eq.astype(F32)
    sel = gt | (eq & (eq_before < need[None]))
    _tile_prefix(sel, il_ref, to_ref, tot_ref)
    toff_ref[...] = to_ref[...].astype(jnp.int32)

    reps = cap // ln
    ones = jnp.ones((ln, ln), BF16)

    def to_column(vals, bf16_exact=False):
        out = None
        for piece in ((vals.astype(BF16),) if bf16_exact else _split3_bf16(vals)):
            t = lax.dot_general(piece, ones, (((0,), (0,)), ((), ())), preferred_element_type=F32)
            out = t if out is None else out + t
        return out

    slot = lax.broadcasted_iota(jnp.int32, (nt, cap), 1).astype(F32)
    tile_id = lax.broadcasted_iota(jnp.int32, (nt, cap), 0).astype(F32)
    in_tile = lax.broadcasted_iota(jnp.int32, (ln, cap), 0).astype(F32)
    for e in range(ne):
        il_e = il_ref[:, e, :]
        to_e = jnp.tile(to_ref[:, e, :], (1, reps))
        tot_e = jnp.tile(tot_ref[:, e, :], (1, reps))
        owns = (to_e <= slot) & (slot < to_e + tot_e)
        rank = jnp.sum(jnp.where(owns, slot - to_e, 0.0), axis=0, keepdims=True)
        tile_of = jnp.sum(jnp.where(owns, tile_id, 0.0), axis=0, keepdims=True)
        owns_b = owns.astype(F32).astype(BF16)
        contract0 = (((0,), (0,)), ((), ()))
        il_g = lax.dot_general(il_e.astype(BF16), owns_b, contract0,
                               preferred_element_type=F32)
        pos = jnp.sum((il_g <= rank).astype(F32), axis=0, keepdims=True)
        idx_ref[e:e + 1, :] = (tile_of * ln + pos).astype(jnp.int32)
        p1, p2, p3 = _split3_bf16(aff_ref[:, e, :])
        a_g = (lax.dot_general(p1, owns_b, contract0, preferred_element_type=F32)
               + lax.dot_general(p2, owns_b, contract0, preferred_element_type=F32)
               + lax.dot_general(p3, owns_b, contract0, preferred_element_type=F32))
        hit = in_tile == pos
        gcol_ref[e] = to_column(jnp.where(hit, a_g, 0.0))
        tok_hi = to_column(jnp.where(hit, tile_of, 0.0), bf16_exact=True)
        tok_lo = to_column(jnp.where(hit, pos, 0.0), bf16_exact=True)
        tokcol_ref[e] = (tok_hi * ln + tok_lo).astype(jnp.int32)


def _route(aff_all, cap):
    s, nt, ne, ln = aff_all.shape
    assert nt <= 256 and cap % ln == 0
    return pl.pallas_call(
        functools.partial(_route_kernel, cap=cap),
        grid=(s,),
        in_specs=[pl.BlockSpec((None, nt, ne, ln), lambda i: (i, 0, 0, 0))],
        out_specs=[pl.BlockSpec((None, ne, cap), lambda i: (i, 0, 0)),
                   pl.BlockSpec((None, ne, cap, ln), lambda i: (i, 0, 0, 0)),
                   pl.BlockSpec((None, ne, cap, ln), lambda i: (i, 0, 0, 0)),
                   pl.BlockSpec((None, nt, ne, ln), lambda i: (i, 0, 0, 0))],
        out_shape=[jax.ShapeDtypeStruct((s, ne, cap), jnp.int32),
                   jax.ShapeDtypeStruct((s, ne, cap, ln), F32),
                   jax.ShapeDtypeStruct((s, ne, cap, ln), jnp.int32),
                   jax.ShapeDtypeStruct((s, nt, ne, ln), jnp.int32)],
        scratch_shapes=[pltpu.VMEM((nt, ne, ln), F32)] * 3,
        compiler_params=_cparams(("parallel",)),
        name="route",
    )(aff_all)


def _route_and_ffn(h2_list, aff_list, w_gate, w_up, w_down, layer):
    n = h2_list[0].shape[0]
    assert all(h.shape[0] == n for h in h2_list)
    cap = EC_CAPACITY * n // N_EXPERTS
    idx_all, gcol, tokcol, toff = _route(jnp.stack(aff_list), cap)
    ye = _expert_ffn(idx_all.reshape(-1), h2_list[0], h2_list[1], w_gate, w_up, w_down, gcol, layer)
    return ye, tokcol, toff


def kernel(x_prompt, x_sample, cache_k, cache_v, c, c_ctx, w_mod, b_mod, norm1_g, norm2_g,
           attn_wqkv, attn_wo, lambda_q1, lambda_k1, lambda_q2, lambda_k2, subln_g,
           pool_w, pool_b, pool_scale, router_w, expert_w_gate, expert_w_up, expert_w_down,
           final_g):
    bp, n_ctx, d = x_prompt.shape
    bs, n_lat, _ = x_sample.shape
    past = cache_k.shape[2]
    n_p, n_s = bp * n_ctx, bs * n_lat

    rows = 8 * ((1 + bs + 7) // 8)
    cvecs = jnp.zeros((rows, d), F32).at[0].set(c_ctx).at[1:1 + bs].set(c)
    mod = _modulation_all(cvecs, w_mod, b_mod)
    rope_tabs = _rope_tables(n_lat)
    cache_k4 = cache_k.reshape(bs, cache_k.shape[1], past, ATTN_WIDTH)
    cache_v4 = cache_v.reshape(bs, cache_v.shape[1], past, ATTN_WIDTH)
    w_gate_b = expert_w_gate.astype(BF16)
    w_up_b = expert_w_up.astype(BF16)
    w_down_b = expert_w_down.astype(BF16)

    xp = x_prompt.reshape(n_p, d)
    xs = x_sample.reshape(n_s, d)
    new_k, new_v = [], []
    for l in range(DEPTH):
        mp_ = mod[l, 0:1].reshape(1, 6, 1, d)
        ms_ = mod[l, 1:1 + bs].reshape(bs, 6, 1, d)
        sp1, cp1, gp1, sp2, cp2, gp2 = [mp_[:, i] for i in range(6)]
        ss1, cs1, gs1, ss2, cs2, gs2 = [ms_[:, i] for i in range(6)]
        wr_t = router_w[l].T.astype(BF16)
        if l % 2 == 0:
            a = l // 2
            lam_init = 0.8 - 0.6 * math.exp(-0.3 * l)
            lamv = jnp.stack([lambda_q1[a], lambda_k1[a], lambda_q2[a], lambda_k2[a]])
            wqkv = attn_wqkv[a].astype(BF16)
            wo = attn_wo[a].astype(BF16)
            qp, kp, vp = _qkv_project(xp, norm1_g[l], sp1, cp1, wqkv, None, n_p, F32)
            new_k.append(kp.reshape(bp, n_ctx, N_HEADS, 2 * HEAD_DIM))
            new_v.append(vp.reshape(bp, n_ctx, N_HEADS, V_DIM))
            op = _diff_attention(qp, kp, vp, lamv, subln_g[a], lam_init, bp)
            xp, h2p, affp = _proj_mixer(op, wo, xp, gp1, norm2_g[l], sp2, cp2, wr_t, n_p)
            qs, ks_, vs = _qkv_project(xs, norm1_g[l], ss1, cs1, wqkv, rope_tabs, n_lat, BF16)
            os_ = _diff_attention(qs, ks_, vs, lamv, subln_g[a], lam_init, bs,
                                  cache=(cache_k4, cache_v4, a))
            xs, h2s, affs = _proj_mixer(os_, wo, xs, gs1, norm2_g[l], ss2, cs2, wr_t, n_lat)
        else:
            p = l // 2
            xp, h2p, affp = _pool_mixer(xp, norm1_g[l], sp1, cp1, pool_w[p], pool_b[p],
                                        pool_scale[p], gp1, norm2_g[l], sp2, cp2, wr_t,
                                        n_p, n_ctx)
            xs, h2s, affs = _pool_mixer(xs, norm1_g[l], ss1, cs1, pool_w[p], pool_b[p],
                                        pool_scale[p], gs1, norm2_g[l], ss2, cs2, wr_t,
                                        n_lat, n_lat)
        ye, tokcol, toff = _route_and_ffn([h2p, h2s], [affp, affs], w_gate_b, w_up_b, w_down_b, l)
        last = l == DEPTH - 1
        xp = _combine(ye, tokcol, toff[0], 0, xp, gp2, final_g, n_p, last)
        xs = _combine(ye, tokcol, toff[1], 1, xs, gs2, final_g, n_lat, last)
    return (xp.reshape(bp, n_ctx, d), xs.reshape(bs, n_lat, d),
            jnp.stack(new_k, axis=1), jnp.stack(new_v, axis=1))
```
